```python
import math
import jax, jax.numpy as jnp
from jax import lax
import numpy as np

D_MODEL = 1024
BATCH = 8
SEQ = 2048
DEPTH = 2
DEC_BATCH = 128
DEC_SEQ = 4
PAST_LEN = 16384
PAGE_SIZE = 128

N_HEADS = 4
HEAD_K = 128
HEAD_V = 128
MIX_K = N_HEADS * HEAD_K
MIX_V = N_HEADS * HEAD_V
CONV_W = 4
CONV_DIM = 2 * MIX_K + MIX_V
CHUNK = 64
N_BRANCH = 3
D_FF = -(-8 * D_MODEL // (3 * 256)) * 256
ROPE_BASE = 10000.0
EPS = 1e-6
F_TINY = 1e-30

_SPLITS = (('gdn_q', MIX_K), ('gdn_k', MIX_K), ('gdn_v', MIX_V), ('gdn_z', MIX_V),
           ('gdn_a', N_HEADS), ('gdn_b', N_HEADS),
           ('ret_q', MIX_K), ('ret_k', MIX_K), ('ret_v', MIX_V), ('ret_g', MIX_V),
           ('hg_q', MIX_K), ('hg_f', MIX_K), ('hg_i', MIX_V), ('hg_g', MIX_V),
           ('merge', N_BRANCH * D_MODEL))
N_IN = sum(s for _, s in _SPLITS)

kernel_name = 'hybrid_gdn_retnet_hgrn2_gated_merge_step'


def _split_cols(proj):
    cols, off = {}, 0
    for name, size in _SPLITS:
        cols[name] = proj[..., off:off + size]
        off += size
    return cols


def _rmsnorm(x, g):
    xf = x.astype(jnp.float32)
    y = xf * lax.rsqrt(jnp.mean(xf * xf, axis=-1, keepdims=True) + EPS)
    return (y * g.astype(jnp.float32)).astype(x.dtype)


def _l2norm(x):
    return x * lax.rsqrt(jnp.sum(x * x, axis=-1, keepdims=True) + EPS)


def _heads(t, d):
    B, L, _ = t.shape
    return t.reshape(B, L, N_HEADS, d).transpose(0, 2, 1, 3)


def _rope(x, pos):
    half = x.shape[-1] // 2
    inv = ROPE_BASE ** (-jnp.arange(half, dtype=jnp.float32) / half)
    ang = pos.astype(jnp.float32)[:, None] * inv[None, :]
    cos, sin = jnp.cos(ang), jnp.sin(ang)
    x1, x2 = x[..., :half], x[..., half:]
    return jnp.concatenate([x1 * cos - x2 * sin, x1 * sin + x2 * cos], axis=-1)


def _masked_exp(mask, z):
    return jnp.where(mask, jnp.exp(jnp.where(mask, z, 0.0)), 0.0)


def _to_chunks(t, n, C):
    B, H = t.shape[0], t.shape[1]
    return jnp.moveaxis(t.reshape(B, H, n, C, *t.shape[3:]), 2, 0)


def _from_chunks(o):
    n, B, H, C, d = o.shape
    return jnp.moveaxis(o, 0, 2).reshape(B, H, n * C, d)


def _gated_delta_chunked(q, k, v, g, beta, S0):
    L = q.shape[2]
    dv = v.shape[-1]
    C = math.gcd(L, CHUNK)
    n = L // C
    incl = jnp.tril(jnp.ones((C, C), bool))
    strict = jnp.tril(jnp.ones((C, C), bool), -1)
    eye = jnp.eye(C, dtype=jnp.float32)

    def step(S, inp):
        qi, ki, vi, gi, bi = inp
        G = jnp.cumsum(gi, axis=-1)
        decay = _masked_exp(incl, G[..., :, None] - G[..., None, :])
        A = jnp.where(strict, bi[..., None] * jnp.einsum('bhtd,bhsd->bhts', ki, ki) * decay, 0.0)
        rhs = jnp.concatenate([bi[..., None] * vi, (bi * jnp.exp(G))[..., None] * ki], axis=-1)
        sol = lax.linalg.triangular_solve(eye + A, rhs, left_side=True, lower=True, unit_diagonal=True)
        u = sol[..., :dv] - jnp.einsum('bhtk,bhkv->bhtv', sol[..., dv:], S)
        scores = jnp.einsum('bhtd,bhsd->bhts', qi, ki) * decay
        o = (jnp.exp(G)[..., None] * jnp.einsum('bhtk,bhkv->bhtv', qi, S)
             + jnp.einsum('bhts,bhsv->bhtv', scores, u))
        k_end = ki * jnp.exp(G[..., -1:] - G)[..., None]
        S_new = jnp.exp(G[..., -1])[..., None, None] * S + jnp.einsum('bhsk,bhsv->bhkv', k_end, u)
        return S_new, o

    xs = tuple(_to_chunks(t, n, C) for t in (q, k, v, g, beta))
    S, o = lax.scan(step, S0, xs)
    return _from_chunks(o), S


def _retention_chunked(q, k, v, log_gamma, S0):
    L = q.shape[2]
    C = math.gcd(L, CHUNK)
    n = L // C
    idx = jnp.arange(C, dtype=jnp.float32)
    rel = idx[:, None] - idx[None, :]
    lg = log_gamma[:, None, None]
    D = _masked_exp(rel >= 0, lg * rel)
    inner = jnp.exp(log_gamma[:, None] * (idx + 1.0))
    to_end = jnp.exp(log_gamma[:, None] * (C - 1.0 - idx))
    chunk_dec = jnp.exp(log_gamma * C)

    def step(S, inp):
        qi, ki, vi = inp
        scores = jnp.einsum('bhtd,bhsd->bhts', qi, ki) * D
        o = (jnp.einsum('bhts,bhsv->bhtv', scores, vi)
             + inner[..., None] * jnp.einsum('bhtk,bhkv->bhtv', qi, S))
        S_new = chunk_dec[:, None, None] * S + jnp.einsum('bhsk,bhsv->bhkv', ki * to_end[..., None], vi)
        return S_new, o

    xs = tuple(_to_chunks(t, n, C) for t in (q, k, v))
    S, o = lax.scan(step, S0, xs)
    return _from_chunks(o), S


def _hgrn2_chunked(q, k, v, log_f, S0):
    L = q.shape[2]
    C = math.gcd(L, CHUNK)
    n = L // C
    incl = jnp.tril(jnp.ones((C, C), bool))[:, :, None]

    def step(S, inp):
        qi, ki, vi, gi = inp
        A = jnp.cumsum(gi, axis=2)
        w = _masked_exp(incl, A[:, :, :, None, :] - A[:, :, None, :, :])
        scores = jnp.einsum('bhtd,bhsd,bhtsd->bhts', qi, ki, w)
        o = (jnp.einsum('bhts,bhsv->bhtv', scores, vi)
             + jnp.einsum('bhtk,bhkv->bhtv', qi * jnp.exp(A), S))
        S_new = (jnp.exp(A[:, :, -1])[..., None] * S
                 + jnp.einsum('bhsk,bhsv->bhkv', ki * jnp.exp(A[:, :, -1:] - A), vi))
        return S_new, o

    xs = tuple(_to_chunks(t, n, C) for t in (q, k, v, log_f))
    S, o = lax.scan(step, S0, xs)
    return _from_chunks(o), S


def _token_mixers(h, s_gdn, s_conv, s_ret, s_hg, pos, lb, w_in, conv_w, a_log, dt_bias,
                  gdn_g, hg_g, w_branch, w_out):
    B, L, _ = h.shape
    f32 = jnp.float32
    proj = h @ w_in
    cols = _split_cols(proj)

    xpad = jnp.concatenate([s_conv.astype(proj.dtype), proj[..., :CONV_DIM]], axis=1)
    conv = xpad[:, 0:L] * conv_w[0]
    for j in range(1, CONV_W):
        conv = conv + xpad[:, j:j + L] * conv_w[j]
    qkv = jax.nn.silu(conv.astype(f32))
    new_conv = xpad[:, L:]
    q_a = _l2norm(_heads(qkv[..., :MIX_K], HEAD_K)) * HEAD_K ** -0.5
    k_a = _l2norm(_heads(qkv[..., MIX_K:2 * MIX_K], HEAD_K))
    v_a = _heads(qkv[..., 2 * MIX_K:], HEAD_V)
    g_a = -jnp.exp(a_log.astype(f32)) * jax.nn.softplus(cols['gdn_a'].astype(f32) + dt_bias.astype(f32))
    beta = jax.nn.sigmoid(cols['gdn_b'].astype(f32))
    o_a, new_gdn = _gated_delta_chunked(q_a, k_a, v_a, g_a.transpose(0, 2, 1), beta.transpose(0, 2, 1),
                                        s_gdn.astype(f32))
    o_a = _rmsnorm(o_a.transpose(0, 2, 1, 3), gdn_g) * jax.nn.silu(
        cols['gdn_z'].astype(f32).reshape(B, L, N_HEADS, HEAD_V))
    o_a = o_a.reshape(B, L, MIX_V)

    q_b = _rope(_heads(cols['ret_q'].astype(f32), HEAD_K), pos)
    k_b = _rope(_heads(cols['ret_k'].astype(f32), HEAD_K), pos) * HEAD_K ** -0.5
    v_b = _heads(cols['ret_v'].astype(f32), HEAD_V)
    log_gamma = jnp.log1p(-jnp.exp2(-5.0 - jnp.arange(N_HEADS, dtype=f32)))
    o_b, new_ret = _retention_chunked(q_b, k_b, v_b, log_gamma, s_ret.astype(f32))
    o_b = o_b.transpose(0, 2, 1, 3)
    mu = jnp.mean(o_b, axis=-1, keepdims=True)
    var = jnp.mean(jnp.square(o_b - mu), axis=-1, keepdims=True)
    o_b = (o_b - mu) * lax.rsqrt(var + EPS) * jax.nn.silu(
        cols['ret_g'].astype(f32).reshape(B, L, N_HEADS, HEAD_V))
    o_b = o_b.reshape(B, L, MIX_V)

    z_f = _heads(cols['hg_f'].astype(f32), HEAD_K)
    lb_h = lb.reshape(N_HEADS, 1, HEAD_K)
    f_gate = jax.nn.sigmoid(z_f) + lb_h * jax.nn.sigmoid(-z_f)
    log_f = jnp.log(jnp.maximum(f_gate, F_TINY))
    k_c = (1.0 - lb_h) * jax.nn.sigmoid(-z_f)
    q_c = _heads(cols['hg_q'].astype(f32), HEAD_K)
    v_c = _heads(cols['hg_i'].astype(f32), HEAD_V)
    o_c, new_hg = _hgrn2_chunked(q_c, k_c, v_c, log_f, s_hg.astype(f32))
    o_c = _rmsnorm(o_c.transpose(0, 2, 1, 3), hg_g) * jax.nn.sigmoid(
        cols['hg_g'].astype(f32).reshape(B, L, N_HEADS, HEAD_V))
    o_c = o_c.reshape(B, L, MIX_V)

    ob = jnp.stack([o_a, o_b, o_c], axis=2).astype(h.dtype)
    branch = jnp.einsum('blnv,nvd->blnd', ob, w_branch)
    gates = jax.nn.sigmoid(cols['merge'].reshape(B, L, N_BRANCH, D_MODEL))
    merged = jnp.sum(gates * branch, axis=2)
    return merged @ w_out, (new_gdn, new_conv, new_ret, new_hg)


def _swiglu(h, w_gate_up, w_down):
    gu = h @ w_gate_up
    return (jax.nn.silu(gu[..., :D_FF]) * gu[..., D_FF:]) @ w_down


def _trunk(x, c, s_gdn, s_conv, s_ret, s_hg, pos0, params):
    (w_in, conv_w, gdn_a_log, gdn_dt_bias, gdn_norm, hgrn_lb, hgrn_norm, w_branch, w_out,
     w_ada, b_ada, norm_mix, norm_ffn, w_gate_up, w_down, final_norm) = params
    L = x.shape[1]
    pos = pos0 + jnp.arange(L, dtype=jnp.int32)
    lb_w = jax.nn.softmax(hgrn_lb.astype(jnp.float32), axis=0)
    lb_all = jnp.cumsum(lb_w, axis=0) - lb_w[0:1]
    cs = jax.nn.silu(c)
    out_gdn, out_conv, out_ret, out_hg = [], [], [], []
    for l in range(DEPTH):
        mod = (cs @ w_ada[l] + b_ada[l])[:, None, :]
        sh_m, sc_m, gt_m, sh_f, sc_f, gt_f = jnp.split(mod, 6, axis=-1)
        h = _rmsnorm(x, norm_mix[l]) * (1.0 + sc_m) + sh_m
        mix, (ng, nc, nr, nh) = _token_mixers(
            h, s_gdn[l], s_conv[l], s_ret[l], s_hg[l], pos, lb_all[l], w_in[l], conv_w[l],
            gdn_a_log[l], gdn_dt_bias[l], gdn_norm[l], hgrn_norm[l], w_branch[l], w_out[l])
        x = x + gt_m * mix.astype(x.dtype)
        h = _rmsnorm(x, norm_ffn[l]) * (1.0 + sc_f) + sh_f
        x = x + gt_f * _swiglu(h, w_gate_up[l], w_down[l])
        out_gdn.append(ng.astype(s_gdn.dtype))
        out_conv.append(nc.astype(s_conv.dtype))
        out_ret.append(nr.astype(s_ret.dtype))
        out_hg.append(nh.astype(s_hg.dtype))
    y = _rmsnorm(x, final_norm)
    return y, (jnp.stack(out_gdn), jnp.stack(out_conv), jnp.stack(out_ret), jnp.stack(out_hg))


def setup_inputs(seed: int = 0) -> dict:
    key = jax.random.key(seed)
    ks = jax.random.split(key, 32)
    f32 = jnp.float32

    def nrm(k, shape, s):
        return jax.random.normal(k, shape, f32) * s

    return {
        'x_prompt': nrm(ks[0], (BATCH, SEQ, D_MODEL), 1.0),
        'x_sample': nrm(ks[1], (DEC_BATCH, DEC_SEQ, D_MODEL), 1.0),
        'state_gdn': nrm(ks[2], (DEPTH, DEC_BATCH, N_HEADS, HEAD_K, HEAD_V), 0.1),
        'state_gdn_conv': nrm(ks[3], (DEPTH, DEC_BATCH, CONV_W - 1, CONV_DIM), 1.0),
        'state_ret': nrm(ks[4], (DEPTH, DEC_BATCH, N_HEADS, HEAD_K, HEAD_V), 0.1),
        'state_hgrn': nrm(ks[5], (DEPTH, DEC_BATCH, N_HEADS, HEAD_K, HEAD_V), 0.1),
        'c_prompt': nrm(ks[6], (BATCH, D_MODEL), 1.0),
        'c_sample': nrm(ks[7], (DEC_BATCH, D_MODEL), 1.0),
        'w_in': nrm(ks[8], (DEPTH, D_MODEL, N_IN), D_MODEL ** -0.5),
        'conv_w': nrm(ks[9], (DEPTH, CONV_W, CONV_DIM), CONV_W ** -0.5),
        'gdn_a_log': jnp.log(jax.random.uniform(ks[10], (DEPTH, N_HEADS), f32, 1.0, 16.0)),
        'gdn_dt_bias': nrm(ks[11], (DEPTH, N_HEADS), 0.1),
        'gdn_norm': 1.0 + nrm(ks[12], (DEPTH, HEAD_V), 0.02),
        'hgrn_lb': nrm(ks[13], (DEPTH, MIX_K), 0.5),
        'hgrn_norm': 1.0 + nrm(ks[14], (DEPTH, HEAD_V), 0.02),
        'w_branch': nrm(ks[15], (DEPTH, N_BRANCH, MIX_V, D_MODEL), MIX_V ** -0.5),
        'w_out': nrm(ks[16], (DEPTH, D_MODEL, D_MODEL), D_MODEL ** -0.5),
        'w_ada': nrm(ks[17], (DEPTH, D_MODEL, 6 * D_MODEL), 0.5 * D_MODEL ** -0.5),
        'b_ada': nrm(ks[18], (DEPTH, 6 * D_MODEL), 0.01),
        'norm_mix': 1.0 + nrm(ks[19], (DEPTH, D_MODEL), 0.02),
        'norm_ffn': 1.0 + nrm(ks[20], (DEPTH, D_MODEL), 0.02),
        'w_gate_up': nrm(ks[21], (DEPTH, D_MODEL, 2 * D_FF), D_MODEL ** -0.5),
        'w_down': nrm(ks[22], (DEPTH, D_FF, D_MODEL), D_FF ** -0.5),
        'final_norm': 1.0 + nrm(ks[23], (D_MODEL,), 0.02),
    }


def reference(x_prompt, x_sample, state_gdn, state_gdn_conv, state_ret, state_hgrn, c_prompt, c_sample,
              w_in, conv_w, gdn_a_log, gdn_dt_bias, gdn_norm, hgrn_lb, hgrn_norm, w_branch, w_out,
              w_ada, b_ada, norm_mix, norm_ffn, w_gate_up, w_down, final_norm):
    params = (w_in, conv_w, gdn_a_log, gdn_dt_bias, gdn_norm, hgrn_lb, hgrn_norm, w_branch, w_out,
              w_ada, b_ada, norm_mix, norm_ffn, w_gate_up, w_down, final_norm)
    Bp = x_prompt.shape[0]
    zs = jnp.zeros((DEPTH, Bp, N_HEADS, HEAD_K, HEAD_V), state_gdn.dtype)
    zc = jnp.zeros((DEPTH, Bp, CONV_W - 1, CONV_DIM), state_gdn_conv.dtype)
    y_prompt, (gdn_p, conv_p, ret_p, hg_p) = _trunk(x_prompt, c_prompt, zs, zc, zs, zs, 0, params)
    y_sample, (gdn_s, conv_s, ret_s, hg_s) = _trunk(x_sample, c_sample, state_gdn, state_gdn_conv,
                                                    state_ret, state_hgrn, PAST_LEN, params)
    return (y_prompt, y_sample, gdn_p, conv_p, ret_p, hg_p, gdn_s, conv_s, ret_s, hg_s)
```

```python
import functools

import jax
import jax.numpy as jnp
from jax import lax
from jax.experimental import pallas as pl
from jax.experimental.pallas import tpu as pltpu

F32 = jnp.float32
BF16 = jnp.bfloat16

D_MODEL = 1024
N_HEADS = 4
HEAD = 128
MIX = N_HEADS * HEAD
CONV_W = 4
CONV_DIM = 3 * MIX
CHUNK = 64
N_BRANCH = 3
D_FF = 2816
ROPE_BASE = 10000.0
EPS = 1e-6
F_TINY = 1e-30
PAST_LEN = 16384

GDN_OFF = 0
RET_OFF = 4 * MIX
HG_OFF = 8 * MIX
MIXER_COLS = 12 * MIX
MERGE_OFF = MIXER_COLS
MAIN_COLS = MIXER_COLS + N_BRANCH * D_MODEL
AB_COLS = 128

SUBLANES = 8
VMEM_LIMIT = 48 * 1024 * 1024

_DN = {
    'nn': (((1,), (0,)), ((), ())),
    'nt': (((1,), (1,)), ((), ())),
    'tn': (((0,), (0,)), ((), ())),
}


def _split2(a):
    hi = a.astype(BF16)
    lo = (a - hi.astype(F32)).astype(BF16)
    return hi, lo


def _dot(a, b, dims='nn', passes=1):
    dn = _DN[dims]
    if passes == 1:
        return lax.dot_general(a.astype(BF16), b.astype(BF16), dn, preferred_element_type=F32)
    a_hi, a_lo = _split2(a)
    b_hi, b_lo = _split2(b)
    out = lax.dot_general(a_hi, b_lo, dn, preferred_element_type=F32)
    out = out + lax.dot_general(a_lo, b_hi, dn, preferred_element_type=F32)
    return out + lax.dot_general(a_hi, b_hi, dn, preferred_element_type=F32)


def _cumsum_rows(ltri_bf16, x):
    x1 = x.astype(BF16)
    r1 = x - x1.astype(F32)
    x2 = r1.astype(BF16)
    x3 = (r1 - x2.astype(F32)).astype(BF16)
    dn = _DN['nn']
    out = lax.dot_general(ltri_bf16, x3, dn, preferred_element_type=F32)
    out = out + lax.dot_general(ltri_bf16, x2, dn, preferred_element_type=F32)
    return out + lax.dot_general(ltri_bf16, x1, dn, preferred_element_type=F32)


def _silu(x):
    return x * jax.nn.sigmoid(x)


def _rms(x):
    return x * lax.rsqrt(jnp.mean(x * x, axis=-1, keepdims=True) + EPS)


def _ada_kernel(c_ref, w_ref, b_ref, o_ref):
    cs = _silu(c_ref[...])
    o_ref[...] = _dot(cs, w_ref[...]) + b_ref[...]


def _ada_call(c_all, w_ada, b_ada):
    depth = w_ada.shape[0]
    rows = c_all.shape[0]
    n_out = w_ada.shape[2]
    tn = 1536
    return pl.pallas_call(
        _ada_kernel,
        grid=(depth, n_out // tn),
        in_specs=[
            pl.BlockSpec((rows, D_MODEL), lambda l, j: (0, 0)),
            pl.BlockSpec((None, D_MODEL, tn), lambda l, j: (l, 0, j)),
            pl.BlockSpec((None, 1, tn), lambda l, j: (l, 0, j)),
        ],
        out_specs=pl.BlockSpec((None, rows, tn), lambda l, j: (l, 0, j)),
        out_shape=jax.ShapeDtypeStruct((depth, rows, n_out), F32),
        compiler_params=pltpu.CompilerParams(
            dimension_semantics=("arbitrary", "arbitrary"), vmem_limit_bytes=VMEM_LIMIT),
        name="ada",
    )(c_all, w_ada, b_ada.reshape(depth, 1, n_out))


def _mod_spec(per_token, tm, tiles_per_seq, seg, ngrid):
    if per_token:
        if ngrid == 2:
            return pl.BlockSpec((tm, D_MODEL), lambda i, j: (i, seg))
        return pl.BlockSpec((tm, D_MODEL), lambda i: (i, seg))
    if ngrid == 2:
        return pl.BlockSpec((None, 1, D_MODEL), lambda i, j: (i // tiles_per_seq, 0, seg))
    return pl.BlockSpec((None, 1, D_MODEL), lambda i: (i // tiles_per_seq, 0, seg))


def _in_kernel(x_ref, sh_ref, sc_ref, nw_ref, w_ref, wab_ref, o_ref, ab_ref, h_scr):
    @pl.when(pl.program_id(1) == 0)
    def _():
        h = _rms(x_ref[...]) * nw_ref[...]
        h = h * (1.0 + sc_ref[...]) + sh_ref[...]
        hb = h.astype(BF16)
        h_scr[...] = hb
        ab_ref[...] = jnp.dot(hb, wab_ref[...], preferred_element_type=F32)

    o_ref[...] = jnp.dot(h_scr[...], w_ref[...], preferred_element_type=F32)


def _in_call(x2, mod, per_token, seq_len, norm_w, w_main, w_ab):
    T = x2.shape[0]
    tm = 512
    tn = 1536
    tps = max(seq_len // tm, 1)
    return pl.pallas_call(
        _in_kernel,
        grid=(T // tm, MAIN_COLS // tn),
        in_specs=[
            pl.BlockSpec((tm, D_MODEL), lambda i, j: (i, 0)),
            _mod_spec(per_token, tm, tps, 0, 2),
            _mod_spec(per_token, tm, tps, 1, 2),
            pl.BlockSpec((1, D_MODEL), lambda i, j: (0, 0)),
            pl.BlockSpec((D_MODEL, tn), lambda i, j: (0, j)),
            pl.BlockSpec((D_MODEL, AB_COLS), lambda i, j: (0, 0)),
        ],
        out_specs=[
            pl.BlockSpec((tm, tn), lambda i, j: (i, j)),
            pl.BlockSpec((tm, AB_COLS), lambda i, j: (i, 0)),
        ],
        out_shape=[
            jax.ShapeDtypeStruct((T, MAIN_COLS), F32),
            jax.ShapeDtypeStruct((T, AB_COLS), F32),
        ],
        scratch_shapes=[pltpu.VMEM((tm, D_MODEL), BF16)],
        compiler_params=pltpu.CompilerParams(
            dimension_semantics=("arbitrary", "arbitrary"), vmem_limit_bytes=VMEM_LIMIT),
        name="in_proj",
    )(x2, mod, mod, norm_w.reshape(1, D_MODEL), w_main, w_ab)


def _mixer_kernel(proj_ref, ab_ref, conv0_ref, sg0_ref, sr0_ref, sh0_ref,
                  cw_ref, hp_ref, gn_ref, hn_ref, lb_ref, cos_ref, sin_ref,
                  dtab_ref, inner_ref, toend_ref, cdec_ref,
                  ob_ref, sg_ref, sr_ref, sh_ref,
                  xpad_ref, hg_ref, *, C, Lv, nb, layer, passes):
    @pl.when(pl.program_id(1) == 0)
    def _():
        sg_ref[...] = sg0_ref[...]
        sr_ref[...] = sr0_ref[...]
        sh_ref[...] = sh0_ref[...]
        xpad_ref[:, 0:SUBLANES, :] = conv0_ref[...]

    padded = Lv < C
    row1 = lax.broadcasted_iota(jnp.int32, (C, 1), 0)
    valid = row1 < Lv
    ri = lax.broadcasted_iota(jnp.int32, (C, C), 0)
    ci = lax.broadcasted_iota(jnp.int32, (C, C), 1)
    incl = ri >= ci
    strict = ri > ci
    eye = ri == ci
    ltri = jnp.where(incl, 1.0, 0.0).astype(BF16)
    r128 = lax.broadcasted_iota(jnp.int32, (HEAD, HEAD), 0)
    c128 = lax.broadcasted_iota(jnp.int32, (HEAD, HEAD), 1)
    eye128 = r128 == c128
    n_levels = max(C.bit_length() - 1, 1)

    lbp = lb_ref[...]
    lbe = jnp.exp(lbp - jnp.max(lbp, axis=0, keepdims=True))
    lbw = lbe / jnp.sum(lbe, axis=0, keepdims=True)
    lb_row = lbw[0:1, :]
    for d in range(1, layer + 1):
        lb_row = lb_row + lbw[d:d + 1, :]
    lb_row = lb_row - lbw[0:1, :]

    neg_a = -jnp.exp(hp_ref[0:1, :])
    dt_b = hp_ref[1:2, :]
    gn = gn_ref[...]
    hn = hn_ref[...]
    cosf = cos_ref[...]
    sinf = sin_ref[...]

    SB = min(16, C)

    for n in range(nb):
        xpad_ref[n, SUBLANES:SUBLANES + C, :] = proj_ref[n, :, GDN_OFF:GDN_OFF + CONV_DIM]
        conv = xpad_ref[n, SUBLANES:SUBLANES + C, :] * cw_ref[CONV_W - 1:CONV_W, :]
        for j in range(CONV_W - 1):
            off = SUBLANES - (CONV_W - 1) + j
            conv = conv + xpad_ref[n, off:off + C, :] * cw_ref[j:j + 1, :]
        xpad_ref[n, 0:SUBLANES, :] = xpad_ref[n, C:C + SUBLANES, :]
        qkv = _silu(conv)

        ab = ab_ref[n]
        g_all = neg_a * jax.nn.softplus(ab + dt_b)
        beta_all = jax.nn.sigmoid(ab)
        if padded:
            g_all = jnp.where(valid, g_all, 0.0)
            beta_all = jnp.where(valid, beta_all, 0.0)
        G_all = _cumsum_rows(ltri, g_all)

        for h in range(N_HEADS):
            q = qkv[:, h * HEAD:(h + 1) * HEAD]
            k = qkv[:, MIX + h * HEAD:MIX + (h + 1) * HEAD]
            v = qkv[:, 2 * MIX + h * HEAD:2 * MIX + (h + 1) * HEAD]
            q = q * lax.rsqrt(jnp.sum(q * q, axis=-1, keepdims=True) + EPS) * (HEAD ** -0.5)
            k = k * lax.rsqrt(jnp.sum(k * k, axis=-1, keepdims=True) + EPS)
            Gc = G_all[:, h:h + 1]
            beta = beta_all[:, N_HEADS + h:N_HEADS + h + 1]
            Gr = jnp.sum(jnp.where(eye, Gc, 0.0), axis=0, keepdims=True)
            G_last = G_all[C - 1:C, h:h + 1]
            eG = jnp.exp(Gc)
            decay = jnp.where(incl, jnp.exp(jnp.where(incl, Gc - Gr, 0.0)), 0.0)
            kk = _dot(k, k, 'nt', passes)
            qk = _dot(q, k, 'nt', passes)
            P = -jnp.where(strict, beta * kk * decay, 0.0)
            sol = jnp.concatenate([beta * v, (beta * eG) * k], axis=-1)
            for lvl in range(n_levels):
                sol = sol + _dot(P, sol, 'nn', passes)
                if lvl + 1 < n_levels:
                    P = _dot(P, P, 'nn', passes)
            S = sg_ref[n, h]
            u = sol[:, :HEAD] - _dot(sol[:, HEAD:], S, 'nn', passes)
            o = eG * _dot(q, S, 'nn', passes) + _dot(qk * decay, u, 'nn', passes)
            k_end = k * jnp.exp(G_last - Gc)
            sg_ref[n, h] = jnp.exp(G_last) * S + _dot(k_end, u, 'tn', passes)
            z = proj_ref[n, :, GDN_OFF + 3 * MIX + h * HEAD:GDN_OFF + 3 * MIX + (h + 1) * HEAD]
            ob_ref[n, :, h * HEAD:(h + 1) * HEAD] = (_rms(o) * gn) * _silu(z)

        for h in range(N_HEADS):
            base = RET_OFF + h * HEAD
            q = proj_ref[n, :, base:base + HEAD]
            k = proj_ref[n, :, base + MIX:base + MIX + HEAD]
            v = proj_ref[n, :, base + 2 * MIX:base + 2 * MIX + HEAD]
            gate = proj_ref[n, :, base + 3 * MIX:base + 3 * MIX + HEAD]
            q = q * cosf + pltpu.roll(q, HEAD // 2, 1) * sinf
            k = (k * cosf + pltpu.roll(k, HEAD // 2, 1) * sinf) * (HEAD ** -0.5)
            if padded:
                v = jnp.where(valid, v, 0.0)
            S = sr_ref[n, h]
            scores = _dot(q, k, 'nt', passes) * dtab_ref[h]
            o = _dot(scores, v, 'nn', passes) + inner_ref[h] * _dot(q, S, 'nn', passes)
            sr_ref[n, h] = cdec_ref[h] * S + _dot(k * toend_ref[h], v, 'tn', passes)
            mu = jnp.mean(o, axis=-1, keepdims=True)
            oc = o - mu
            var = jnp.mean(oc * oc, axis=-1, keepdims=True)
            ob_ref[n, :, MIX + h * HEAD:MIX + (h + 1) * HEAD] = oc * lax.rsqrt(var + EPS) * _silu(gate)

        for h in range(N_HEADS):
            base = HG_OFF + h * HEAD
            qc = proj_ref[n, :, base:base + HEAD]
            zf = proj_ref[n, :, base + MIX:base + MIX + HEAD]
            vc = proj_ref[n, :, base + 2 * MIX:base + 2 * MIX + HEAD]
            gate = proj_ref[n, :, base + 3 * MIX:base + 3 * MIX + HEAD]
            lb_h = lb_row[:, h * HEAD:(h + 1) * HEAD]
            s_neg = jax.nn.sigmoid(-zf)
            f_gate = jax.nn.sigmoid(zf) + lb_h * s_neg
            log_f = jnp.log(jnp.maximum(f_gate, F_TINY))
            kc = (1.0 - lb_h) * s_neg
            if padded:
                log_f = jnp.where(valid, log_f, 0.0)
                kc = jnp.where(valid, kc, 0.0)
                vc = jnp.where(valid, vc, 0.0)
            A = _cumsum_rows(ltri, log_f)
            hg_ref[0] = A
            hg_ref[1] = kc
            score_rows = []
            for blk in range(C // SB):
                r0 = blk * SB
                qI = qc[r0:r0 + SB]
                AI = A[r0:r0 + SB]
                rowg = r0 + lax.broadcasted_iota(jnp.int32, (SB, C), 0)
                colg = lax.broadcasted_iota(jnp.int32, (SB, C), 1)
                rowl = lax.broadcasted_iota(jnp.int32, (SB, 1), 0)
                sc_blk = jnp.zeros((SB, C), F32)
                for s in range(SB):
                    a_s = hg_ref[0, r0 + s:r0 + s + 1, :]
                    k_s = hg_ref[1, r0 + s:r0 + s + 1, :]
                    w = jnp.exp(jnp.where(rowl >= s, AI - a_s, 0.0))
                    val = jnp.sum(qI * k_s * w, axis=-1, keepdims=True)
                    sc_blk = jnp.where(colg == r0 + s, val, sc_blk)
                if r0 > 0:
                    a_b = hg_ref[0, r0 - 1:r0, :]
                    before = row1 < r0
                    kt = jnp.where(before, kc * jnp.exp(jnp.where(before, a_b - A, 0.0)), 0.0)
                    qt = qI * jnp.exp(AI - a_b)
                    sc_blk = sc_blk + _dot(qt, kt, 'nt', passes)
                score_rows.append(jnp.where(rowg >= colg, sc_blk, 0.0))
            scores = score_rows[0] if len(score_rows) == 1 else jnp.concatenate(score_rows, axis=0)
            S = sh_ref[n, h]
            a_last = hg_ref[0, C - 1:C, :]
            o = _dot(scores, vc, 'nn', passes) + _dot(qc * jnp.exp(A), S, 'nn', passes)
            dec_col = jnp.sum(jnp.where(eye128, jnp.exp(a_last), 0.0), axis=1, keepdims=True)
            sh_ref[n, h] = dec_col * S + _dot(kc * jnp.exp(a_last - A), vc, 'tn', passes)
            ob_ref[n, :, 2 * MIX + h * HEAD:2 * MIX + (h + 1) * HEAD] = (_rms(o) * hn) * jax.nn.sigmoid(gate)


def _mixer_call(proj3, ab3, conv0, sg0, sr0, sh0, conv_w, head_params, gdn_norm, hg_norm, hgrn_lb,
                cosf, sinf, dtab, inner, toend, cdec, *, C, Lv, nb, layer, passes):
    B, Lp, _ = proj3.shape
    n_chunks = Lp // C
    depth = hgrn_lb.shape[0]
    state_spec = pl.BlockSpec((nb, N_HEADS, HEAD, HEAD), lambda b, i: (b, 0, 0, 0))
    state_shape = jax.ShapeDtypeStruct((B, N_HEADS, HEAD, HEAD), F32)

    def whole(shape):
        return pl.BlockSpec(shape, lambda b, i: (0,) * len(shape))

    kern = functools.partial(_mixer_kernel, C=C, Lv=Lv, nb=nb, layer=layer, passes=passes)
    return pl.pallas_call(
        kern,
        grid=(B // nb, n_chunks),
        in_specs=[
            pl.BlockSpec((nb, C, MIXER_COLS), lambda b, i: (b, i, 0)),
            pl.BlockSpec((nb, C, AB_COLS), lambda b, i: (b, i, 0)),
            pl.BlockSpec((nb, SUBLANES, CONV_DIM), lambda b, i: (b, 0, 0)),
            state_spec, state_spec, state_spec,
            whole((CONV_W, CONV_DIM)),
            whole((2, AB_COLS)),
            whole((1, HEAD)),
            whole((1, HEAD)),
            whole((depth, MIX)),
            pl.BlockSpec((C, HEAD), lambda b, i: (i, 0)),
            pl.BlockSpec((C, HEAD), lambda b, i: (i, 0)),
            whole((N_HEADS, C, C)),
            whole((N_HEADS, C, HEAD)),
            whole((N_HEADS, C, HEAD)),
            whole((N_HEADS, 1, HEAD)),
        ],
        out_specs=[
            pl.BlockSpec((nb, C, N_BRANCH * MIX), lambda b, i: (b, i, 0)),
            state_spec, state_spec, state_spec,
        ],
        out_shape=[
            jax.ShapeDtypeStruct((B, Lp, N_BRANCH * MIX), F32),
            state_shape, state_shape, state_shape,
        ],
        scratch_shapes=[
            pltpu.VMEM((nb, C + SUBLANES, CONV_DIM), F32),
            pltpu.VMEM((2, C, HEAD), F32),
        ],
        compiler_params=pltpu.CompilerParams(
            dimension_semantics=("arbitrary", "arbitrary"), vmem_limit_bytes=VMEM_LIMIT),
        name="mixer",
    )(proj3, ab3, conv0, sg0, sr0, sh0, conv_w, head_params, gdn_norm.reshape(1, HEAD),
      hg_norm.reshape(1, HEAD), hgrn_lb, cosf, sinf, dtab, inner, toend, cdec)


def _post_kernel(ob_ref, mg_ref, x_ref, gt_ref, wb_ref, wo_ref, o_ref):
    merged = None
    for n in range(N_BRANCH):
        br = _dot(ob_ref[:, n * MIX:(n + 1) * MIX], wb_ref[n])
        term = jax.nn.sigmoid(mg_ref[:, n * D_MODEL:(n + 1) * D_MODEL]) * br
        merged = term if merged is None else merged + term
    o_ref[...] = x_ref[...] + gt_ref[...] * _dot(merged, wo_ref[...])


def _post_call(ob2, proj, x2, mod, per_token, seq_len, w_branch, w_out):
    T = x2.shape[0]
    tm = 256
    tps = max(seq_len // tm, 1)
    merge_blk = MERGE_OFF // (N_BRANCH * D_MODEL)
    return pl.pallas_call(
        _post_kernel,
        grid=(T // tm,),
        in_specs=[
            pl.BlockSpec((tm, N_BRANCH * MIX), lambda i: (i, 0)),
            pl.BlockSpec((tm, N_BRANCH * D_MODEL), lambda i: (i, merge_blk)),
            pl.BlockSpec((tm, D_MODEL), lambda i: (i, 0)),
            _mod_spec(per_token, tm, tps, 2, 1),
            pl.BlockSpec((N_BRANCH, MIX, D_MODEL), lambda i: (0, 0, 0)),
            pl.BlockSpec((D_MODEL, D_MODEL), lambda i: (0, 0)),
        ],
        out_specs=pl.BlockSpec((tm, D_MODEL), lambda i: (i, 0)),
        out_shape=jax.ShapeDtypeStruct((T, D_MODEL), F32),
        compiler_params=pltpu.CompilerParams(
            dimension_semantics=("arbitrary",), vmem_limit_bytes=VMEM_LIMIT),
        name="post",
    )(ob2, proj, x2, mod, w_branch, w_out)


def _ffn_kernel(x_ref, sh_ref, sc_ref, gt_ref, nw_ref, fw_ref, wg_ref, wu_ref, wd_ref, o_ref,
                h_scr, acc_scr, *, final):
    j = pl.program_id(1)

    @pl.when(j == 0)
    def _():
        h = _rms(x_ref[...]) * nw_ref[...]
        h_scr[...] = (h * (1.0 + sc_ref[...]) + sh_ref[...]).astype(BF16)
        acc_scr[...] = jnp.zeros_like(acc_scr)

    h = h_scr[...]
    g = jnp.dot(h, wg_ref[...], preferred_element_type=F32)
    u = jnp.dot(h, wu_ref[...], preferred_element_type=F32)
    acc_scr[...] += _dot(_silu(g) * u, wd_ref[...])

    @pl.when(j == pl.num_programs(1) - 1)
    def _():
        xn = x_ref[...] + gt_ref[...] * acc_scr[...]
        if final:
            xn = _rms(xn) * fw_ref[...]
        o_ref[...] = xn


def _ffn_call(x2, mod, per_token, seq_len, norm_w, final_w, w_gate_up, w_down, final):
    T = x2.shape[0]
    tm = 512
    tf = D_FF // 2
    tps = max(seq_len // tm, 1)
    n_ff = D_FF // tf
    return pl.pallas_call(
        functools.partial(_ffn_kernel, final=final),
        grid=(T // tm, n_ff),
        in_specs=[
            pl.BlockSpec((tm, D_MODEL), lambda i, j: (i, 0)),
            _mod_spec(per_token, tm, tps, 3, 2),
            _mod_spec(per_token, tm, tps, 4, 2),
            _mod_spec(per_token, tm, tps, 5, 2),
            pl.BlockSpec((1, D_MODEL), lambda i, j: (0, 0)),
            pl.BlockSpec((1, D_MODEL), lambda i, j: (0, 0)),
            pl.BlockSpec((D_MODEL, tf), lambda i, j: (0, j)),
            pl.BlockSpec((D_MODEL, tf), lambda i, j: (0, n_ff + j)),
            pl.BlockSpec((tf, D_MODEL), lambda i, j: (j, 0)),
        ],
        out_specs=pl.BlockSpec((tm, D_MODEL), lambda i, j: (i, 0)),
        out_shape=jax.ShapeDtypeStruct((T, D_MODEL), F32),
        scratch_shapes=[pltpu.VMEM((tm, D_MODEL), BF16), pltpu.VMEM((tm, D_MODEL), F32)],
        compiler_params=pltpu.CompilerParams(
            dimension_semantics=("arbitrary", "arbitrary"), vmem_limit_bytes=VMEM_LIMIT),
        name="ffn",
    )(x2, mod, mod, mod, norm_w.reshape(1, D_MODEL), final_w.reshape(1, D_MODEL),
      w_gate_up, w_gate_up, w_down)


def _rope_tables(pos0, length, padded_len):
    half = HEAD // 2
    pos = pos0 + jnp.arange(padded_len, dtype=jnp.int32)
    inv = ROPE_BASE ** (-jnp.arange(half, dtype=F32) / half)
    ang = pos.astype(F32)[:, None] * inv[None, :]
    cos, sin = jnp.cos(ang), jnp.sin(ang)
    return jnp.concatenate([cos, cos], axis=-1), jnp.concatenate([-sin, sin], axis=-1)


def _retention_tables(c_real, c_pad):
    log_gamma = jnp.log1p(-jnp.exp2(-5.0 - jnp.arange(N_HEADS, dtype=F32)))
    idx = jnp.arange(c_real, dtype=F32)
    rel = idx[:, None] - idx[None, :]
    mask = rel >= 0
    lg = log_gamma[:, None, None]
    dtab = jnp.where(mask, jnp.exp(jnp.where(mask, lg * rel, 0.0)), 0.0)
    inner = jnp.exp(log_gamma[:, None] * (idx + 1.0))
    to_end = jnp.exp(log_gamma[:, None] * (c_real - 1.0 - idx))
    cdec = jnp.exp(log_gamma * c_real)
    p = c_pad - c_real
    dtab = jnp.pad(dtab, ((0, 0), (0, p), (0, p)))
    inner = jnp.broadcast_to(jnp.pad(inner, ((0, 0), (0, p)))[:, :, None], (N_HEADS, c_pad, HEAD))
    to_end = jnp.broadcast_to(jnp.pad(to_end, ((0, 0), (0, p)))[:, :, None], (N_HEADS, c_pad, HEAD))
    cdec = jnp.broadcast_to(cdec[:, None, None], (N_HEADS, 1, HEAD))
    return dtab, inner, to_end, cdec


def _trunk(x, mod_all, states, pos0, wts, nb, passes):
    B, L, _ = x.shape
    depth = mod_all.shape[0]
    c_real = CHUNK if L % CHUNK == 0 else L
    assert L % c_real == 0 and c_real <= CHUNK
    C = -(-c_real // SUBLANES) * SUBLANES
    Lp = L if C == c_real else C
    assert Lp == L or L == c_real
    per_token = Lp < 256
    if Lp != L:
        x = jnp.pad(x, ((0, 0), (0, Lp - L), (0, 0)))
    T = B * Lp
    x2 = x.reshape(T, D_MODEL)
    cosf, sinf = _rope_tables(pos0, L, Lp)
    dtab, inner, toend, cdec = _retention_tables(c_real, C)

    outs = ([], [], [], [])
    for l in range(depth):
        if per_token:
            mod = jnp.repeat(mod_all[l], Lp, axis=0)
        else:
            mod = mod_all[l].reshape(B, 1, 6 * D_MODEL)
        if states is None:
            zs = jnp.zeros((B, N_HEADS, HEAD, HEAD), F32)
            sg0 = sr0 = sh0 = zs
            conv0 = jnp.zeros((B, SUBLANES, CONV_DIM), F32)
        else:
            sg0, sr0, sh0 = states[0][l], states[2][l], states[3][l]
            conv0 = jnp.pad(states[1][l], ((0, 0), (SUBLANES - (CONV_W - 1), 0), (0, 0)))
        proj, ab = _in_call(x2, mod, per_token, Lp, wts['norm_mix'][l], wts['w_main'][l], wts['w_ab'][l])
        proj3 = proj.reshape(B, Lp, MAIN_COLS)
        ob, sg, sr, sh = _mixer_call(
            proj3, ab.reshape(B, Lp, AB_COLS), conv0, sg0, sr0, sh0,
            wts['conv_w'][l], wts['head_params'][l], wts['gdn_norm'][l], wts['hgrn_norm'][l], wts['hgrn_lb'],
            cosf, sinf, dtab, inner, toend, cdec, C=C, Lv=c_real, nb=nb, layer=l, passes=passes)
        tail = proj3[:, L - min(L, CONV_W - 1):L, GDN_OFF:GDN_OFF + CONV_DIM]
        if L < CONV_W - 1:
            tail = jnp.concatenate([states[1][l][:, L:], tail], axis=1)
        outs[0].append(sg)
        outs[1].append(tail)
        outs[2].append(sr)
        outs[3].append(sh)
        x1 = _post_call(ob.reshape(T, N_BRANCH * MIX), proj, x2, mod, per_token, Lp,
                        wts['w_branch'][l], wts['w_out'][l])
        x2 = _ffn_call(x1, mod, per_token, Lp, wts['norm_ffn'][l], wts['final_norm'],
                       wts['w_gate_up'][l], wts['w_down'][l], final=(l == depth - 1))
    y = x2.reshape(B, Lp, D_MODEL)[:, :L]
    return y, tuple(jnp.stack(o) for o in outs)


def _prep_weights(w_in, conv_w, gdn_a_log, gdn_dt_bias, gdn_norm, hgrn_lb, hgrn_norm, w_branch, w_out,
                  norm_mix, norm_ffn, w_gate_up, w_down, final_norm):
    ab0 = 4 * MIX
    w_main = jnp.concatenate([w_in[:, :, :ab0], w_in[:, :, ab0 + 2 * N_HEADS:]], axis=-1).astype(BF16)
    w_ab = jnp.pad(w_in[:, :, ab0:ab0 + 2 * N_HEADS],
                   ((0, 0), (0, 0), (0, AB_COLS - 2 * N_HEADS))).astype(BF16)
    head_params = jnp.pad(jnp.stack([gdn_a_log, gdn_dt_bias], axis=1),
                          ((0, 0), (0, 0), (0, AB_COLS - N_HEADS))).astype(F32)
    return dict(
        w_main=w_main, w_ab=w_ab, conv_w=conv_w, head_params=head_params, gdn_norm=gdn_norm,
        hgrn_norm=hgrn_norm, hgrn_lb=hgrn_lb, w_branch=w_branch.astype(BF16), w_out=w_out.astype(BF16),
        norm_mix=norm_mix, norm_ffn=norm_ffn, w_gate_up=w_gate_up.astype(BF16),
        w_down=w_down.astype(BF16), final_norm=final_norm)


def kernel(x_prompt, x_sample, state_gdn, state_gdn_conv, state_ret, state_hgrn, c_prompt, c_sample,
           w_in, conv_w, gdn_a_log, gdn_dt_bias, gdn_norm, hgrn_lb, hgrn_norm, w_branch, w_out,
           w_ada, b_ada, norm_mix, norm_ffn, w_gate_up, w_down, final_norm):
    Bp = x_prompt.shape[0]
    wts = _prep_weights(w_in, conv_w, gdn_a_log, gdn_dt_bias, gdn_norm, hgrn_lb, hgrn_norm, w_branch, w_out,
                        norm_mix, norm_ffn, w_gate_up, w_down, final_norm)
    c_all = jnp.concatenate([c_prompt, c_sample], axis=0)
    mod_all = _ada_call(c_all, w_ada, b_ada)
    y_p, st_p = _trunk(x_prompt, mod_all[:, :Bp], None, 0, wts, nb=1, passes=3)
    y_s, st_s = _trunk(x_sample, mod_all[:, Bp:], (state_gdn, state_gdn_conv, state_ret, state_hgrn),
                       PAST_LEN, wts, nb=1, passes=3)
    return (y_p, y_s) + st_p + st_s
```

```python
import functools

import jax
import jax.numpy as jnp
from jax import lax
from jax.experimental import pallas as pl
from jax.experimental.pallas import tpu as pltpu

F32 = jnp.float32
BF16 = jnp.bfloat16

D_MODEL = 1024
N_HEADS = 4
HEAD = 128
MIX = N_HEADS * HEAD
CONV_W = 4
CONV_DIM = 3 * MIX
CHUNK = 64
N_BRANCH = 3
D_FF = 2816
ROPE_BASE = 10000.0
EPS = 1e-6
F_TINY = 1e-30
PAST_LEN = 16384

GDN_OFF = 0
RET_OFF = 4 * MIX
HG_OFF = 8 * MIX
MIXER_COLS = 12 * MIX
MERGE_OFF = MIXER_COLS
MAIN_COLS = MIXER_COLS + N_BRANCH * D_MODEL
AB_COLS = 128

SUBLANES = 8
VMEM_LIMIT = 48 * 1024 * 1024

_DN = {
    'nn': (((1,), (0,)), ((), ())),
    'nt': (((1,), (1,)), ((), ())),
    'tn': (((0,), (0,)), ((), ())),
}


def _split2(a):
    hi = a.astype(BF16)
    lo = (a - hi.astype(F32)).astype(BF16)
    return hi, lo


def _dot(a, b, dims='nn', passes=1):
    dn = _DN[dims]
    if passes == 1:
        return lax.dot_general(a.astype(BF16), b.astype(BF16), dn, preferred_element_type=F32)
    a_hi, a_lo = _split2(a)
    b_hi, b_lo = _split2(b)
    out = lax.dot_general(a_hi, b_lo, dn, preferred_element_type=F32)
    out = out + lax.dot_general(a_lo, b_hi, dn, preferred_element_type=F32)
    return out + lax.dot_general(a_hi, b_hi, dn, preferred_element_type=F32)


def _cumsum_rows(ltri_bf16, x):
    x1 = x.astype(BF16)
    r1 = x - x1.astype(F32)
    x2 = r1.astype(BF16)
    x3 = (r1 - x2.astype(F32)).astype(BF16)
    dn = _DN['nn']
    out = lax.dot_general(ltri_bf16, x3, dn, preferred_element_type=F32)
    out = out + lax.dot_general(ltri_bf16, x2, dn, preferred_element_type=F32)
    return out + lax.dot_general(ltri_bf16, x1, dn, preferred_element_type=F32)


def _silu(x):
    return x * jax.nn.sigmoid(x)


def _rms(x):
    return x * lax.rsqrt(jnp.mean(x * x, axis=-1, keepdims=True) + EPS)


def _ada_kernel(c_ref, w_ref, b_ref, o_ref):
    cs = _silu(c_ref[...])
    o_ref[...] = _dot(cs, w_ref[...]) + b_ref[...]


def _ada_call(c_all, w_ada, b_ada):
    depth = w_ada.shape[0]
    rows = c_all.shape[0]
    n_out = w_ada.shape[2]
    tn = 1536
    return pl.pallas_call(
        _ada_kernel,
        grid=(depth, n_out // tn),
        in_specs=[
            pl.BlockSpec((rows, D_MODEL), lambda l, j: (0, 0)),
            pl.BlockSpec((None, D_MODEL, tn), lambda l, j: (l, 0, j)),
            pl.BlockSpec((None, 1, tn), lambda l, j: (l, 0, j)),
        ],
        out_specs=pl.BlockSpec((None, rows, tn), lambda l, j: (l, 0, j)),
        out_shape=jax.ShapeDtypeStruct((depth, rows, n_out), F32),
        compiler_params=pltpu.CompilerParams(
            dimension_semantics=("arbitrary", "arbitrary"), vmem_limit_bytes=VMEM_LIMIT),
        name="ada",
    )(c_all, w_ada, b_ada.reshape(depth, 1, n_out))


def _mod_spec(per_token, tm, tiles_per_seq, seg, ngrid):
    if per_token:
        if ngrid == 2:
            return pl.BlockSpec((tm, D_MODEL), lambda i, j: (i, seg))
        return pl.BlockSpec((tm, D_MODEL), lambda i: (i, seg))
    if ngrid == 2:
        return pl.BlockSpec((None, 1, D_MODEL), lambda i, j: (i // tiles_per_seq, 0, seg))
    return pl.BlockSpec((None, 1, D_MODEL), lambda i: (i // tiles_per_seq, 0, seg))


def _in_kernel(x_ref, sh_ref, sc_ref, nw_ref, w_ref, wab_ref, o_ref, ab_ref, h_scr):
    @pl.when(pl.program_id(1) == 0)
    def _():
        h = _rms(x_ref[...]) * nw_ref[...]
        h = h * (1.0 + sc_ref[...]) + sh_ref[...]
        hb = h.astype(BF16)
        h_scr[...] = hb
        ab_ref[...] = jnp.dot(hb, wab_ref[...], preferred_element_type=F32)

    o_ref[...] = jnp.dot(h_scr[...], w_ref[...], preferred_element_type=F32)


def _in_call(x2, mod, per_token, seq_len, norm_w, w_main, w_ab):
    T = x2.shape[0]
    tm = 512
    tn = 1536
    tps = max(seq_len // tm, 1)
    return pl.pallas_call(
        _in_kernel,
        grid=(T // tm, MAIN_COLS // tn),
        in_specs=[
            pl.BlockSpec((tm, D_MODEL), lambda i, j: (i, 0)),
            _mod_spec(per_token, tm, tps, 0, 2),
            _mod_spec(per_token, tm, tps, 1, 2),
            pl.BlockSpec((1, D_MODEL), lambda i, j: (0, 0)),
            pl.BlockSpec((D_MODEL, tn), lambda i, j: (0, j)),
            pl.BlockSpec((D_MODEL, AB_COLS), lambda i, j: (0, 0)),
        ],
        out_specs=[
            pl.BlockSpec((tm, tn), lambda i, j: (i, j)),
            pl.BlockSpec((tm, AB_COLS), lambda i, j: (i, 0)),
        ],
        out_shape=[
            jax.ShapeDtypeStruct((T, MAIN_COLS), F32),
            jax.ShapeDtypeStruct((T, AB_COLS), F32),
        ],
        scratch_shapes=[pltpu.VMEM((tm, D_MODEL), BF16)],
        compiler_params=pltpu.CompilerParams(
            dimension_semantics=("arbitrary", "arbitrary"), vmem_limit_bytes=VMEM_LIMIT),
        name="in_proj",
    )(x2, mod, mod, norm_w.reshape(1, D_MODEL), w_main, w_ab)


def _mixer_kernel(proj_ref, ab_ref, conv0_ref, sg0_ref, sr0_ref, sh0_ref,
                  cw_ref, hp_ref, gn_ref, hn_ref, lb_ref, cos_ref, sin_ref,
                  dtab_ref, inner_ref, toend_ref, cdec_ref,
                  ob_ref, sg_ref, sr_ref, sh_ref,
                  xpad_ref, hg_ref, *, C, Lv, nb, layer, passes):
    @pl.when(pl.program_id(1) == 0)
    def _():
        sg_ref[...] = sg0_ref[...]
        sr_ref[...] = sr0_ref[...]
        sh_ref[...] = sh0_ref[...]
        xpad_ref[:, 0:SUBLANES, :] = conv0_ref[...]

    padded = Lv < C
    row1 = lax.broadcasted_iota(jnp.int32, (C, 1), 0)
    valid = row1 < Lv
    ri = lax.broadcasted_iota(jnp.int32, (C, C), 0)
    ci = lax.broadcasted_iota(jnp.int32, (C, C), 1)
    incl = ri >= ci
    strict = ri > ci
    eye = ri == ci
    ltri = jnp.where(incl, 1.0, 0.0).astype(BF16)
    r128 = lax.broadcasted_iota(jnp.int32, (HEAD, HEAD), 0)
    c128 = lax.broadcasted_iota(jnp.int32, (HEAD, HEAD), 1)
    eye128 = r128 == c128
    n_levels = max((Lv - 1).bit_length(), 1)

    lbp = lb_ref[...]
    lbe = jnp.exp(lbp - jnp.max(lbp, axis=0, keepdims=True))
    lbw = lbe / jnp.sum(lbe, axis=0, keepdims=True)
    lb_row = lbw[0:1, :]
    for d in range(1, layer + 1):
        lb_row = lb_row + lbw[d:d + 1, :]
    lb_row = lb_row - lbw[0:1, :]

    neg_a = -jnp.exp(hp_ref[0:1, :])
    dt_b = hp_ref[1:2, :]
    gn = gn_ref[...]
    hn = hn_ref[...]
    cosf = cos_ref[...]
    sinf = sin_ref[...]

    SB = SUBLANES

    def gdn_task(n, h, qkv, G_all, beta_all):
        q = qkv[:, h * HEAD:(h + 1) * HEAD]
        k = qkv[:, MIX + h * HEAD:MIX + (h + 1) * HEAD]
        v = qkv[:, 2 * MIX + h * HEAD:2 * MIX + (h + 1) * HEAD]
        q = q * lax.rsqrt(jnp.sum(q * q, axis=-1, keepdims=True) + EPS) * (HEAD ** -0.5)
        k = k * lax.rsqrt(jnp.sum(k * k, axis=-1, keepdims=True) + EPS)
        kk = _dot(k, k, 'nt', passes)
        qk = _dot(q, k, 'nt', passes)
        yield
        Gc = G_all[:, h:h + 1]
        beta = beta_all[:, N_HEADS + h:N_HEADS + h + 1]
        Gr = jnp.sum(jnp.where(eye, Gc, 0.0), axis=0, keepdims=True)
        G_last = G_all[C - 1:C, h:h + 1]
        eG = jnp.exp(Gc)
        decay = jnp.where(incl, jnp.exp(jnp.where(incl, Gc - Gr, 0.0)), 0.0)
        P = -jnp.where(strict, beta * kk * decay, 0.0)
        sol = jnp.concatenate([beta * v, (beta * eG) * k], axis=-1)
        for lvl in range(n_levels):
            upd = _dot(P, sol, 'nn', passes)
            if lvl + 1 < n_levels:
                P = _dot(P, P, 'nn', passes)
            yield
            sol = sol + upd
        S = sg_ref[n, h]
        sk_s = _dot(sol[:, HEAD:], S, 'nn', passes)
        q_s = _dot(q, S, 'nn', passes)
        yield
        u = sol[:, :HEAD] - sk_s
        k_end = k * jnp.exp(G_last - Gc)
        intra = _dot(qk * decay, u, 'nn', passes)
        s_upd = _dot(k_end, u, 'tn', passes)
        yield
        o = eG * q_s + intra
        sg_ref[n, h] = jnp.exp(G_last) * S + s_upd
        z = proj_ref[n, :, GDN_OFF + 3 * MIX + h * HEAD:GDN_OFF + 3 * MIX + (h + 1) * HEAD]
        ob_ref[n, :, h * HEAD:(h + 1) * HEAD] = (_rms(o) * gn) * _silu(z)

    def ret_task(n, h):
        base = RET_OFF + h * HEAD
        q = proj_ref[n, :, base:base + HEAD]
        k = proj_ref[n, :, base + MIX:base + MIX + HEAD]
        v = proj_ref[n, :, base + 2 * MIX:base + 2 * MIX + HEAD]
        q = q * cosf + pltpu.roll(q, HEAD // 2, 1) * sinf
        k = (k * cosf + pltpu.roll(k, HEAD // 2, 1) * sinf) * (HEAD ** -0.5)
        if padded:
            v = jnp.where(valid, v, 0.0)
        S = sr_ref[n, h]
        qk = _dot(q, k, 'nt', passes)
        q_s = _dot(q, S, 'nn', passes)
        s_upd = _dot(k * toend_ref[h], v, 'tn', passes)
        yield
        intra = _dot(qk * dtab_ref[h], v, 'nn', passes)
        sr_ref[n, h] = cdec_ref[h] * S + s_upd
        yield
        o = intra + inner_ref[h] * q_s
        gate = proj_ref[n, :, base + 3 * MIX:base + 3 * MIX + HEAD]
        mu = jnp.mean(o, axis=-1, keepdims=True)
        oc = o - mu
        var = jnp.mean(oc * oc, axis=-1, keepdims=True)
        ob_ref[n, :, MIX + h * HEAD:MIX + (h + 1) * HEAD] = oc * lax.rsqrt(var + EPS) * _silu(gate)

    def hg_task(n, h):
        base = HG_OFF + h * HEAD
        qc = proj_ref[n, :, base:base + HEAD]
        zf = proj_ref[n, :, base + MIX:base + MIX + HEAD]
        vc = proj_ref[n, :, base + 2 * MIX:base + 2 * MIX + HEAD]
        lb_h = lb_row[:, h * HEAD:(h + 1) * HEAD]
        s_neg = jax.nn.sigmoid(-zf)
        f_gate = jax.nn.sigmoid(zf) + lb_h * s_neg
        log_f = jnp.log(jnp.maximum(f_gate, F_TINY))
        kc = (1.0 - lb_h) * s_neg
        if padded:
            log_f = jnp.where(valid, log_f, 0.0)
            kc = jnp.where(valid, kc, 0.0)
            vc = jnp.where(valid, vc, 0.0)
        A = _cumsum_rows(ltri, log_f)
        yield
        hgs = hg_ref.at[n, h]
        hgs[0] = A
        hgs[1] = kc
        S = sh_ref[n, h]
        a_last = hgs[0, C - 1:C, :]
        q_s = _dot(qc * jnp.exp(A), S, 'nn', passes)
        s_upd = _dot(kc * jnp.exp(a_last - A), vc, 'tn', passes)
        score_rows = []
        for blk in range(C // SB):
            r0 = blk * SB
            qI = qc[r0:r0 + SB]
            AI = A[r0:r0 + SB]
            rowg = r0 + lax.broadcasted_iota(jnp.int32, (SB, C), 0)
            colg = lax.broadcasted_iota(jnp.int32, (SB, C), 1)
            rowl = lax.broadcasted_iota(jnp.int32, (SB, 1), 0)
            cross = None
            if r0 > 0:
                a_b = hgs[0, r0 - 1:r0, :]
                before = row1 < r0
                kt = jnp.where(before, kc * jnp.exp(jnp.where(before, a_b - A, 0.0)), 0.0)
                qt = qI * jnp.exp(AI - a_b)
                cross = _dot(qt, kt, 'nt', passes)
            sc_blk = jnp.zeros((SB, C), F32)
            for s in range(SB):
                a_s = hgs[0, r0 + s:r0 + s + 1, :]
                k_s = hgs[1, r0 + s:r0 + s + 1, :]
                w = jnp.exp(jnp.where(rowl >= s, AI - a_s, 0.0))
                val = jnp.sum(qI * k_s * w, axis=-1, keepdims=True)
                sc_blk = jnp.where(colg == r0 + s, val, sc_blk)
            yield
            if cross is not None:
                sc_blk = sc_blk + cross
            score_rows.append(jnp.where(rowg >= colg, sc_blk, 0.0))
        scores = score_rows[0] if len(score_rows) == 1 else jnp.concatenate(score_rows, axis=0)
        intra = _dot(scores, vc, 'nn', passes)
        dec_col = jnp.sum(jnp.where(eye128, jnp.exp(a_last), 0.0), axis=1, keepdims=True)
        sh_ref[n, h] = dec_col * S + s_upd
        yield
        o = intra + q_s
        gate = proj_ref[n, :, base + 3 * MIX:base + 3 * MIX + HEAD]
        ob_ref[n, :, 2 * MIX + h * HEAD:2 * MIX + (h + 1) * HEAD] = (_rms(o) * hn) * jax.nn.sigmoid(gate)

    gdn_tasks, other_tasks = [], []
    for n in range(nb):
        xpad_ref[n, SUBLANES:SUBLANES + C, :] = proj_ref[n, :, GDN_OFF:GDN_OFF + CONV_DIM]
        conv = xpad_ref[n, SUBLANES:SUBLANES + C, :] * cw_ref[CONV_W - 1:CONV_W, :]
        for j in range(CONV_W - 1):
            off = SUBLANES - (CONV_W - 1) + j
            conv = conv + xpad_ref[n, off:off + C, :] * cw_ref[j:j + 1, :]
        xpad_ref[n, 0:SUBLANES, :] = xpad_ref[n, C:C + SUBLANES, :]
        qkv = _silu(conv)
        ab = ab_ref[n]
        g_all = neg_a * jax.nn.softplus(ab + dt_b)
        beta_all = jax.nn.sigmoid(ab)
        if padded:
            g_all = jnp.where(valid, g_all, 0.0)
            beta_all = jnp.where(valid, beta_all, 0.0)
        G_all = _cumsum_rows(ltri, g_all)
        for h in range(N_HEADS):
            gdn_tasks.append(gdn_task(n, h, qkv, G_all, beta_all))
            other_tasks.append(hg_task(n, h))
            other_tasks.append(ret_task(n, h))
    tasks = gdn_tasks + other_tasks
    while tasks:
        alive = []
        for t in tasks:
            try:
                next(t)
                alive.append(t)
            except StopIteration:
                pass
        tasks = alive


def _mixer_call(proj3, ab3, conv0, sg0, sr0, sh0, conv_w, head_params, gdn_norm, hg_norm, hgrn_lb,
                cosf, sinf, dtab, inner, toend, cdec, *, C, Lv, nb, layer, passes):
    B, Lp, _ = proj3.shape
    n_chunks = Lp // C
    depth = hgrn_lb.shape[0]
    state_spec = pl.BlockSpec((nb, N_HEADS, HEAD, HEAD), lambda b, i: (b, 0, 0, 0))
    state_shape = jax.ShapeDtypeStruct((B, N_HEADS, HEAD, HEAD), F32)

    def whole(shape):
        return pl.BlockSpec(shape, lambda b, i: (0,) * len(shape))

    kern = functools.partial(_mixer_kernel, C=C, Lv=Lv, nb=nb, layer=layer, passes=passes)
    return pl.pallas_call(
        kern,
        grid=(B // nb, n_chunks),
        in_specs=[
            pl.BlockSpec((nb, C, MIXER_COLS), lambda b, i: (b, i, 0)),
            pl.BlockSpec((nb, C, AB_COLS), lambda b, i: (b, i, 0)),
            pl.BlockSpec((nb, SUBLANES, CONV_DIM), lambda b, i: (b, 0, 0)),
            state_spec, state_spec, state_spec,
            whole((CONV_W, CONV_DIM)),
            whole((2, AB_COLS)),
            whole((1, HEAD)),
            whole((1, HEAD)),
            whole((depth, MIX)),
            pl.BlockSpec((C, HEAD), lambda b, i: (i, 0)),
            pl.BlockSpec((C, HEAD), lambda b, i: (i, 0)),
            whole((N_HEADS, C, C)),
            whole((N_HEADS, C, HEAD)),
            whole((N_HEADS, C, HEAD)),
            whole((N_HEADS, 1, HEAD)),
        ],
        out_specs=[
            pl.BlockSpec((nb, C, N_BRANCH * MIX), lambda b, i: (b, i, 0)),
            state_spec, state_spec, state_spec,
        ],
        out_shape=[
            jax.ShapeDtypeStruct((B, Lp, N_BRANCH * MIX), F32),
            state_shape, state_shape, state_shape,
        ],
        scratch_shapes=[
            pltpu.VMEM((nb, C + SUBLANES, CONV_DIM), F32),
            pltpu.VMEM((nb, N_HEADS, 2, C, HEAD), F32),
        ],
        compiler_params=pltpu.CompilerParams(
            dimension_semantics=("arbitrary", "arbitrary"), vmem_limit_bytes=VMEM_LIMIT),
        name="mixer",
    )(proj3, ab3, conv0, sg0, sr0, sh0, conv_w, head_params, gdn_norm.reshape(1, HEAD),
      hg_norm.reshape(1, HEAD), hgrn_lb, cosf, sinf, dtab, inner, toend, cdec)


def _post_kernel(ob_ref, mg_ref, x_ref, gt_ref, wb_ref, wo_ref, o_ref):
    merged = None
    for n in range(N_BRANCH):
        br = _dot(ob_ref[:, n * MIX:(n + 1) * MIX], wb_ref[n])
        term = jax.nn.sigmoid(mg_ref[:, n * D_MODEL:(n + 1) * D_MODEL]) * br
        merged = term if merged is None else merged + term
    o_ref[...] = x_ref[...] + gt_ref[...] * _dot(merged, wo_ref[...])


def _post_call(ob2, proj, x2, mod, per_token, seq_len, w_branch, w_out):
    T = x2.shape[0]
    tm = 256
    tps = max(seq_len // tm, 1)
    merge_blk = MERGE_OFF // (N_BRANCH * D_MODEL)
    return pl.pallas_call(
        _post_kernel,
        grid=(T // tm,),
        in_specs=[
            pl.BlockSpec((tm, N_BRANCH * MIX), lambda i: (i, 0)),
            pl.BlockSpec((tm, N_BRANCH * D_MODEL), lambda i: (i, merge_blk)),
            pl.BlockSpec((tm, D_MODEL), lambda i: (i, 0)),
            _mod_spec(per_token, tm, tps, 2, 1),
            pl.BlockSpec((N_BRANCH, MIX, D_MODEL), lambda i: (0, 0, 0)),
            pl.BlockSpec((D_MODEL, D_MODEL), lambda i: (0, 0)),
        ],
        out_specs=pl.BlockSpec((tm, D_MODEL), lambda i: (i, 0)),
        out_shape=jax.ShapeDtypeStruct((T, D_MODEL), F32),
        compiler_params=pltpu.CompilerParams(
            dimension_semantics=("arbitrary",), vmem_limit_bytes=VMEM_LIMIT),
        name="post",
    )(ob2, proj, x2, mod, w_branch, w_out)


def _ffn_kernel(x_ref, sh_ref, sc_ref, gt_ref, nw_ref, fw_ref, wg_ref, wu_ref, wd_ref, o_ref,
                h_scr, acc_scr, *, final):
    j = pl.program_id(1)

    @pl.when(j == 0)
    def _():
        h = _rms(x_ref[...]) * nw_ref[...]
        h_scr[...] = (h * (1.0 + sc_ref[...]) + sh_ref[...]).astype(BF16)
        acc_scr[...] = jnp.zeros_like(acc_scr)

    h = h_scr[...]
    g = jnp.dot(h, wg_ref[...], preferred_element_type=F32)
    u = jnp.dot(h, wu_ref[...], preferred_element_type=F32)
    acc_scr[...] += _dot(_silu(g) * u, wd_ref[...])

    @pl.when(j == pl.num_programs(1) - 1)
    def _():
        xn = x_ref[...] + gt_ref[...] * acc_scr[...]
        if final:
            xn = _rms(xn) * fw_ref[...]
        o_ref[...] = xn


def _ffn_call(x2, mod, per_token, seq_len, norm_w, final_w, w_gate_up, w_down, final):
    T = x2.shape[0]
    tm = 512
    tf = D_FF // 2
    tps = max(seq_len // tm, 1)
    n_ff = D_FF // tf
    return pl.pallas_call(
        functools.partial(_ffn_kernel, final=final),
        grid=(T // tm, n_ff),
        in_specs=[
            pl.BlockSpec((tm, D_MODEL), lambda i, j: (i, 0)),
            _mod_spec(per_token, tm, tps, 3, 2),
            _mod_spec(per_token, tm, tps, 4, 2),
            _mod_spec(per_token, tm, tps, 5, 2),
            pl.BlockSpec((1, D_MODEL), lambda i, j: (0, 0)),
            pl.BlockSpec((1, D_MODEL), lambda i, j: (0, 0)),
            pl.BlockSpec((D_MODEL, tf), lambda i, j: (0, j)),
            pl.BlockSpec((D_MODEL, tf), lambda i, j: (0, n_ff + j)),
            pl.BlockSpec((tf, D_MODEL), lambda i, j: (j, 0)),
        ],
        out_specs=pl.BlockSpec((tm, D_MODEL), lambda i, j: (i, 0)),
        out_shape=jax.ShapeDtypeStruct((T, D_MODEL), F32),
        scratch_shapes=[pltpu.VMEM((tm, D_MODEL), BF16), pltpu.VMEM((tm, D_MODEL), F32)],
        compiler_params=pltpu.CompilerParams(
            dimension_semantics=("arbitrary", "arbitrary"), vmem_limit_bytes=VMEM_LIMIT),
        name="ffn",
    )(x2, mod, mod, mod, norm_w.reshape(1, D_MODEL), final_w.reshape(1, D_MODEL),
      w_gate_up, w_gate_up, w_down)


def _rope_tables(pos0, length, padded_len):
    half = HEAD // 2
    pos = pos0 + jnp.arange(padded_len, dtype=jnp.int32)
    inv = ROPE_BASE ** (-jnp.arange(half, dtype=F32) / half)
    ang = pos.astype(F32)[:, None] * inv[None, :]
    cos, sin = jnp.cos(ang), jnp.sin(ang)
    return jnp.concatenate([cos, cos], axis=-1), jnp.concatenate([-sin, sin], axis=-1)


def _retention_tables(c_real, c_pad):
    log_gamma = jnp.log1p(-jnp.exp2(-5.0 - jnp.arange(N_HEADS, dtype=F32)))
    idx = jnp.arange(c_real, dtype=F32)
    rel = idx[:, None] - idx[None, :]
    mask = rel >= 0
    lg = log_gamma[:, None, None]
    dtab = jnp.where(mask, jnp.exp(jnp.where(mask, lg * rel, 0.0)), 0.0)
    inner = jnp.exp(log_gamma[:, None] * (idx + 1.0))
    to_end = jnp.exp(log_gamma[:, None] * (c_real - 1.0 - idx))
    cdec = jnp.exp(log_gamma * c_real)
    p = c_pad - c_real
    dtab = jnp.pad(dtab, ((0, 0), (0, p), (0, p)))
    inner = jnp.broadcast_to(jnp.pad(inner, ((0, 0), (0, p)))[:, :, None], (N_HEADS, c_pad, HEAD))
    to_end = jnp.broadcast_to(jnp.pad(to_end, ((0, 0), (0, p)))[:, :, None], (N_HEADS, c_pad, HEAD))
    cdec = jnp.broadcast_to(cdec[:, None, None], (N_HEADS, 1, HEAD))
    return dtab, inner, to_end, cdec


def _trunk(x, mod_all, states, pos0, wts, nb, passes):
    B, L, _ = x.shape
    depth = mod_all.shape[0]
    c_real = CHUNK if L % CHUNK == 0 else L
    assert L % c_real == 0 and c_real <= CHUNK
    C = -(-c_real // SUBLANES) * SUBLANES
    Lp = L if C == c_real else C
    assert Lp == L or L == c_real
    per_token = Lp < 256
    if Lp != L:
        x = jnp.pad(x, ((0, 0), (0, Lp - L), (0, 0)))
    T = B * Lp
    x2 = x.reshape(T, D_MODEL)
    cosf, sinf = _rope_tables(pos0, L, Lp)
    dtab, inner, toend, cdec = _retention_tables(c_real, C)

    outs = ([], [], [], [])
    for l in range(depth):
        if per_token:
            mod = jnp.repeat(mod_all[l], Lp, axis=0)
        else:
            mod = mod_all[l].reshape(B, 1, 6 * D_MODEL)
        if states is None:
            zs = jnp.zeros((B, N_HEADS, HEAD, HEAD), F32)
            sg0 = sr0 = sh0 = zs
            conv0 = jnp.zeros((B, SUBLANES, CONV_DIM), F32)
        else:
            sg0, sr0, sh0 = states[0][l], states[2][l], states[3][l]
            conv0 = jnp.pad(states[1][l], ((0, 0), (SUBLANES - (CONV_W - 1), 0), (0, 0)))
        proj, ab = _in_call(x2, mod, per_token, Lp, wts['norm_mix'][l], wts['w_main'][l], wts['w_ab'][l])
        proj3 = proj.reshape(B, Lp, MAIN_COLS)
        ob, sg, sr, sh = _mixer_call(
            proj3, ab.reshape(B, Lp, AB_COLS), conv0, sg0, sr0, sh0,
            wts['conv_w'][l], wts['head_params'][l], wts['gdn_norm'][l], wts['hgrn_norm'][l], wts['hgrn_lb'],
            cosf, sinf, dtab, inner, toend, cdec, C=C, Lv=c_real, nb=nb, layer=l, passes=passes)
        tail = proj3[:, L - min(L, CONV_W - 1):L, GDN_OFF:GDN_OFF + CONV_DIM]
        if L < CONV_W - 1:
            tail = jnp.concatenate([states[1][l][:, L:], tail], axis=1)
        outs[0].append(sg)
        outs[1].append(tail)
        outs[2].append(sr)
        outs[3].append(sh)
        x1 = _post_call(ob.reshape(T, N_BRANCH * MIX), proj, x2, mod, per_token, Lp,
                        wts['w_branch'][l], wts['w_out'][l])
        x2 = _ffn_call(x1, mod, per_token, Lp, wts['norm_ffn'][l], wts['final_norm'],
                       wts['w_gate_up'][l], wts['w_down'][l], final=(l == depth - 1))
    y = x2.reshape(B, Lp, D_MODEL)[:, :L]
    return y, tuple(jnp.stack(o) for o in outs)


def _prep_weights(w_in, conv_w, gdn_a_log, gdn_dt_bias, gdn_norm, hgrn_lb, hgrn_norm, w_branch, w_out,
                  norm_mix, norm_ffn, w_gate_up, w_down, final_norm):
    ab0 = 4 * MIX
    w_main = jnp.concatenate([w_in[:, :, :ab0], w_in[:, :, ab0 + 2 * N_HEADS:]], axis=-1).astype(BF16)
    w_ab = jnp.pad(w_in[:, :, ab0:ab0 + 2 * N_HEADS],
                   ((0, 0), (0, 0), (0, AB_COLS - 2 * N_HEADS))).astype(BF16)
    head_params = jnp.pad(jnp.stack([gdn_a_log, gdn_dt_bias], axis=1),
                          ((0, 0), (0, 0), (0, AB_COLS - N_HEADS))).astype(F32)
    return dict(
        w_main=w_main, w_ab=w_ab, conv_w=conv_w, head_params=head_params, gdn_norm=gdn_norm,
        hgrn_norm=hgrn_norm, hgrn_lb=hgrn_lb, w_branch=w_branch.astype(BF16), w_out=w_out.astype(BF16),
        norm_mix=norm_mix, norm_ffn=norm_ffn, w_gate_up=w_gate_up.astype(BF16),
        w_down=w_down.astype(BF16), final_norm=final_norm)


def kernel(x_prompt, x_sample, state_gdn, state_gdn_conv, state_ret, state_hgrn, c_prompt, c_sample,
           w_in, conv_w, gdn_a_log, gdn_dt_bias, gdn_norm, hgrn_lb, hgrn_norm, w_branch, w_out,
           w_ada, b_ada, norm_mix, norm_ffn, w_gate_up, w_down, final_norm):
    Bp = x_prompt.shape[0]
    wts = _prep_weights(w_in, conv_w, gdn_a_log, gdn_dt_bias, gdn_norm, hgrn_lb, hgrn_norm, w_branch, w_out,
                        norm_mix, norm_ffn, w_gate_up, w_down, final_norm)
    c_all = jnp.concatenate([c_prompt, c_sample], axis=0)
    mod_all = _ada_call(c_all, w_ada, b_ada)
    y_p, st_p = _trunk(x_prompt, mod_all[:, :Bp], None, 0, wts, nb=2, passes=1)
    y_s, st_s = _trunk(x_sample, mod_all[:, Bp:], (state_gdn, state_gdn_conv, state_ret, state_hgrn),
                       PAST_LEN, wts, nb=8, passes=1)
    return (y_p, y_s) + st_p + st_s
```

```python
import functools

import jax
import jax.numpy as jnp
from jax import lax
from jax.experimental import pallas as pl
from jax.experimental.pallas import tpu as pltpu

F32 = jnp.float32
BF16 = jnp.bfloat16

D_MODEL = 1024
N_HEADS = 4
HEAD = 128
MIX = N_HEADS * HEAD
CONV_W = 4
CONV_DIM = 3 * MIX
CHUNK = 64
N_BRANCH = 3
D_FF = 2816
ROPE_BASE = 10000.0
EPS = 1e-6
F_TINY = 1e-30
PAST_LEN = 16384

GDN_OFF = 0
RET_OFF = 4 * MIX
HG_OFF = 8 * MIX
MIXER_COLS = 12 * MIX
MERGE_OFF = MIXER_COLS
MAIN_COLS = MIXER_COLS + N_BRANCH * D_MODEL
AB_COLS = 128

SUBLANES = 8
VMEM_LIMIT = 56 * 1024 * 1024

_DN = {
    'nn': (((1,), (0,)), ((), ())),
    'nt': (((1,), (1,)), ((), ())),
    'tn': (((0,), (0,)), ((), ())),
}


def _split2(a):
    hi = a.astype(BF16)
    lo = (a - hi.astype(F32)).astype(BF16)
    return hi, lo


def _dot(a, b, dims='nn', passes=1):
    dn = _DN[dims]
    if passes == 1:
        return lax.dot_general(a.astype(BF16), b.astype(BF16), dn, preferred_element_type=F32)
    a_hi, a_lo = _split2(a)
    b_hi, b_lo = _split2(b)
    out = lax.dot_general(a_hi, b_lo, dn, preferred_element_type=F32)
    out = out + lax.dot_general(a_lo, b_hi, dn, preferred_element_type=F32)
    return out + lax.dot_general(a_hi, b_hi, dn, preferred_element_type=F32)


def _cumsum_rows(ltri_bf16, x):
    x1 = x.astype(BF16)
    r1 = x - x1.astype(F32)
    x2 = r1.astype(BF16)
    x3 = (r1 - x2.astype(F32)).astype(BF16)
    dn = _DN['nn']
    out = lax.dot_general(ltri_bf16, x3, dn, preferred_element_type=F32)
    out = out + lax.dot_general(ltri_bf16, x2, dn, preferred_element_type=F32)
    return out + lax.dot_general(ltri_bf16, x1, dn, preferred_element_type=F32)


def _silu(x):
    return x * jax.nn.sigmoid(x)


def _rms(x):
    return x * lax.rsqrt(jnp.mean(x * x, axis=-1, keepdims=True) + EPS)


def _ada_kernel(c_ref, w_ref, b_ref, o_ref):
    cs = _silu(c_ref[...])
    o_ref[...] = _dot(cs, w_ref[...]) + b_ref[...]


def _ada_call(c_all, w_ada, b_ada):
    depth = w_ada.shape[0]
    rows = c_all.shape[0]
    n_out = w_ada.shape[2]
    tn = 1536
    return pl.pallas_call(
        _ada_kernel,
        grid=(depth, n_out // tn),
        in_specs=[
            pl.BlockSpec((rows, D_MODEL), lambda l, j: (0, 0)),
            pl.BlockSpec((None, D_MODEL, tn), lambda l, j: (l, 0, j)),
            pl.BlockSpec((None, 1, tn), lambda l, j: (l, 0, j)),
        ],
        out_specs=pl.BlockSpec((None, rows, tn), lambda l, j: (l, 0, j)),
        out_shape=jax.ShapeDtypeStruct((depth, rows, n_out), F32),
        compiler_params=pltpu.CompilerParams(
            dimension_semantics=("arbitrary", "arbitrary"), vmem_limit_bytes=VMEM_LIMIT),
        name="ada",
    )(c_all, w_ada, b_ada.reshape(depth, 1, n_out))


def _row_tile(n_rows, preferred):
    tm = preferred
    while n_rows % tm:
        tm //= 2
    return tm


def _mod_spec(per_token, tm, tiles_per_seq, seg, ngrid):
    if per_token:
        if ngrid == 2:
            return pl.BlockSpec((tm, D_MODEL), lambda i, j: (i, seg))
        return pl.BlockSpec((tm, D_MODEL), lambda i: (i, seg))
    if ngrid == 2:
        return pl.BlockSpec((None, 1, D_MODEL), lambda i, j: (i // tiles_per_seq, 0, seg))
    return pl.BlockSpec((None, 1, D_MODEL), lambda i: (i // tiles_per_seq, 0, seg))


def _in_kernel(x_ref, sh_ref, sc_ref, nw_ref, w_ref, wab_ref, o_ref, ab_ref, h_scr):
    @pl.when(pl.program_id(1) == 0)
    def _():
        h = _rms(x_ref[...]) * nw_ref[...]
        h = h * (1.0 + sc_ref[...]) + sh_ref[...]
        hb = h.astype(BF16)
        h_scr[...] = hb
        ab_ref[...] = jnp.dot(hb, wab_ref[...], preferred_element_type=F32)

    o_ref[...] = jnp.dot(h_scr[...], w_ref[...], preferred_element_type=F32)


def _in_call(x2, mod, per_token, seq_len, norm_w, w_main, w_ab):
    T = x2.shape[0]
    tm = _row_tile(T if per_token else seq_len, 2048)
    tn = 768
    tps = max(seq_len // tm, 1)
    return pl.pallas_call(
        _in_kernel,
        grid=(T // tm, MAIN_COLS // tn),
        in_specs=[
            pl.BlockSpec((tm, D_MODEL), lambda i, j: (i, 0)),
            _mod_spec(per_token, tm, tps, 0, 2),
            _mod_spec(per_token, tm, tps, 1, 2),
            pl.BlockSpec((1, D_MODEL), lambda i, j: (0, 0)),
            pl.BlockSpec((D_MODEL, tn), lambda i, j: (0, j)),
            pl.BlockSpec((D_MODEL, AB_COLS), lambda i, j: (0, 0)),
        ],
        out_specs=[
            pl.BlockSpec((tm, tn), lambda i, j: (i, j)),
            pl.BlockSpec((tm, AB_COLS), lambda i, j: (i, 0)),
        ],
        out_shape=[
            jax.ShapeDtypeStruct((T, MAIN_COLS), F32),
            jax.ShapeDtypeStruct((T, AB_COLS), F32),
        ],
        scratch_shapes=[pltpu.VMEM((tm, D_MODEL), BF16)],
        compiler_params=pltpu.CompilerParams(
            dimension_semantics=("arbitrary", "arbitrary"), vmem_limit_bytes=VMEM_LIMIT),
        name="in_proj",
    )(x2, mod, mod, norm_w.reshape(1, D_MODEL), w_main, w_ab)


def _mixer_kernel(*refs, C, Lv, nb, layer, passes, has_state):
    proj_ref, ab_ref = refs[:2]
    pos = 2
    if has_state:
        conv0_ref, sg0_ref, sr0_ref, sh0_ref = refs[pos:pos + 4]
        pos += 4
    (cw_ref, hp_ref, gn_ref, hn_ref, lb_ref, cos_ref, sin_ref,
     dtab_ref, inner_ref, toend_ref, cdec_ref) = refs[pos:pos + 11]
    ob_ref, sg_ref, sr_ref, sh_ref, xpad_ref, hg_ref = refs[pos + 11:]
    tail0 = SUBLANES - (CONV_W - 1)

    @pl.when(pl.program_id(1) == 0)
    def _():
        if has_state:
            sg_ref[...] = sg0_ref[...]
            sr_ref[...] = sr0_ref[...]
            sh_ref[...] = sh0_ref[...]
            xpad_ref[:, tail0:SUBLANES, :] = conv0_ref[...]
        else:
            sg_ref[...] = jnp.zeros_like(sg_ref)
            sr_ref[...] = jnp.zeros_like(sr_ref)
            sh_ref[...] = jnp.zeros_like(sh_ref)
            xpad_ref[:, tail0:SUBLANES, :] = jnp.zeros((nb, CONV_W - 1, CONV_DIM), F32)

    padded = Lv < C
    row1 = lax.broadcasted_iota(jnp.int32, (C, 1), 0)
    valid = row1 < Lv
    ri = lax.broadcasted_iota(jnp.int32, (C, C), 0)
    ci = lax.broadcasted_iota(jnp.int32, (C, C), 1)
    incl = ri >= ci
    strict = ri > ci
    eye = ri == ci
    ltri = jnp.where(incl, 1.0, 0.0).astype(BF16)
    r128 = lax.broadcasted_iota(jnp.int32, (HEAD, HEAD), 0)
    c128 = lax.broadcasted_iota(jnp.int32, (HEAD, HEAD), 1)
    eye128 = r128 == c128
    n_levels = max((Lv - 1).bit_length(), 1)

    lbp = lb_ref[...]
    lbe = jnp.exp(lbp - jnp.max(lbp, axis=0, keepdims=True))
    lbw = lbe / jnp.sum(lbe, axis=0, keepdims=True)
    lb_row = lbw[0:1, :]
    for d in range(1, layer + 1):
        lb_row = lb_row + lbw[d:d + 1, :]
    lb_row = lb_row - lbw[0:1, :]

    neg_a = -jnp.exp(hp_ref[0:1, :])
    dt_b = hp_ref[1:2, :]
    gn = gn_ref[...]
    hn = hn_ref[...]
    cosf = cos_ref[...]
    sinf = sin_ref[...]

    SB = SUBLANES

    def gdn_task(n, h, qkv, G_all, beta_all):
        q = qkv[:, h * HEAD:(h + 1) * HEAD]
        k = qkv[:, MIX + h * HEAD:MIX + (h + 1) * HEAD]
        v = qkv[:, 2 * MIX + h * HEAD:2 * MIX + (h + 1) * HEAD]
        q = q * lax.rsqrt(jnp.sum(q * q, axis=-1, keepdims=True) + EPS) * (HEAD ** -0.5)
        k = k * lax.rsqrt(jnp.sum(k * k, axis=-1, keepdims=True) + EPS)
        kk = _dot(k, k, 'nt', passes)
        qk = _dot(q, k, 'nt', passes)
        yield
        Gc = G_all[:, h:h + 1]
        beta = beta_all[:, N_HEADS + h:N_HEADS + h + 1]
        Gr = jnp.sum(jnp.where(eye, Gc, 0.0), axis=0, keepdims=True)
        G_last = G_all[C - 1:C, h:h + 1]
        eG = jnp.exp(Gc)
        decay = jnp.where(incl, jnp.exp(jnp.where(incl, Gc - Gr, 0.0)), 0.0)
        P = -jnp.where(strict, beta * kk * decay, 0.0)
        sol = jnp.concatenate([beta * v, (beta * eG) * k], axis=-1)
        for lvl in range(n_levels):
            upd = _dot(P, sol, 'nn', passes)
            if lvl + 1 < n_levels:
                P = _dot(P, P, 'nn', passes)
            yield
            sol = sol + upd
        S = sg_ref[n, h]
        sk_s = _dot(sol[:, HEAD:], S, 'nn', passes)
        q_s = _dot(q, S, 'nn', passes)
        yield
        u = sol[:, :HEAD] - sk_s
        k_end = k * jnp.exp(G_last - Gc)
        intra = _dot(qk * decay, u, 'nn', passes)
        s_upd = _dot(k_end, u, 'tn', passes)
        yield
        o = eG * q_s + intra
        sg_ref[n, h] = jnp.exp(G_last) * S + s_upd
        z = proj_ref[n, :, GDN_OFF + 3 * MIX + h * HEAD:GDN_OFF + 3 * MIX + (h + 1) * HEAD]
        ob_ref[n, :, h * HEAD:(h + 1) * HEAD] = ((_rms(o) * gn) * _silu(z)).astype(ob_ref.dtype)

    def ret_task(n, h):
        base = RET_OFF + h * HEAD
        q = proj_ref[n, :, base:base + HEAD]
        k = proj_ref[n, :, base + MIX:base + MIX + HEAD]
        v = proj_ref[n, :, base + 2 * MIX:base + 2 * MIX + HEAD]
        q = q * cosf + pltpu.roll(q, HEAD // 2, 1) * sinf
        k = (k * cosf + pltpu.roll(k, HEAD // 2, 1) * sinf) * (HEAD ** -0.5)
        if padded:
            v = jnp.where(valid, v, 0.0)
        S = sr_ref[n, h]
        qk = _dot(q, k, 'nt', passes)
        q_s = _dot(q, S, 'nn', passes)
        s_upd = _dot(k * toend_ref[h], v, 'tn', passes)
        yield
        intra = _dot(qk * dtab_ref[h], v, 'nn', passes)
        sr_ref[n, h] = cdec_ref[h] * S + s_upd
        yield
        o = intra + inner_ref[h] * q_s
        gate = proj_ref[n, :, base + 3 * MIX:base + 3 * MIX + HEAD]
        mu = jnp.mean(o, axis=-1, keepdims=True)
        oc = o - mu
        var = jnp.mean(oc * oc, axis=-1, keepdims=True)
        ob_ref[n, :, MIX + h * HEAD:MIX + (h + 1) * HEAD] = (
            oc * lax.rsqrt(var + EPS) * _silu(gate)).astype(ob_ref.dtype)

    def hg_task(n, h):
        base = HG_OFF + h * HEAD
        qc = proj_ref[n, :, base:base + HEAD]
        zf = proj_ref[n, :, base + MIX:base + MIX + HEAD]
        vc = proj_ref[n, :, base + 2 * MIX:base + 2 * MIX + HEAD]
        lb_h = lb_row[:, h * HEAD:(h + 1) * HEAD]
        s_neg = jax.nn.sigmoid(-zf)
        f_gate = jax.nn.sigmoid(zf) + lb_h * s_neg
        log_f = jnp.log(jnp.maximum(f_gate, F_TINY))
        kc = (1.0 - lb_h) * s_neg
        if padded:
            log_f = jnp.where(valid, log_f, 0.0)
            kc = jnp.where(valid, kc, 0.0)
            vc = jnp.where(valid, vc, 0.0)
        A = _cumsum_rows(ltri, log_f)
        yield
        hgs = hg_ref.at[n, h]
        hgs[0] = A
        hgs[1] = kc
        S = sh_ref[n, h]
        a_last = hgs[0, C - 1:C, :]
        q_s = _dot(qc * jnp.exp(A), S, 'nn', passes)
        s_upd = _dot(kc * jnp.exp(a_last - A), vc, 'tn', passes)
        score_rows = []
        for blk in range(C // SB):
            r0 = blk * SB
            qI = qc[r0:r0 + SB]
            AI = A[r0:r0 + SB]
            rowg = r0 + lax.broadcasted_iota(jnp.int32, (SB, C), 0)
            colg = lax.broadcasted_iota(jnp.int32, (SB, C), 1)
            cross = None
            if r0 > 0:
                a_b = hgs[0, r0 - 1:r0, :]
                before = row1 < r0
                kt = jnp.where(before, kc * jnp.exp(jnp.where(before, a_b - A, 0.0)), 0.0)
                qt = qI * jnp.exp(AI - a_b)
                cross = _dot(qt, kt, 'nt', passes)
            sc_blk = jnp.zeros((SB, C), F32)
            for s in range(SB):
                a_s = hgs[0, r0 + s:r0 + s + 1, :]
                k_s = hgs[1, r0 + s:r0 + s + 1, :]
                w = jnp.exp(AI - a_s)
                val = jnp.sum(qI * k_s * w, axis=-1, keepdims=True)
                sc_blk = jnp.where(colg == r0 + s, val, sc_blk)
            yield
            if cross is not None:
                sc_blk = sc_blk + cross
            score_rows.append(jnp.where(rowg >= colg, sc_blk, 0.0))
        scores = score_rows[0] if len(score_rows) == 1 else jnp.concatenate(score_rows, axis=0)
        intra = _dot(scores, vc, 'nn', passes)
        dec_col = jnp.sum(jnp.where(eye128, jnp.exp(a_last), 0.0), axis=1, keepdims=True)
        sh_ref[n, h] = dec_col * S + s_upd
        yield
        o = intra + q_s
        gate = proj_ref[n, :, base + 3 * MIX:base + 3 * MIX + HEAD]
        ob_ref[n, :, 2 * MIX + h * HEAD:2 * MIX + (h + 1) * HEAD] = (
            (_rms(o) * hn) * jax.nn.sigmoid(gate)).astype(ob_ref.dtype)

    gdn_tasks, other_tasks = [], []
    for n in range(nb):
        xpad_ref[n, SUBLANES:SUBLANES + C, :] = proj_ref[n, :, GDN_OFF:GDN_OFF + CONV_DIM]
        conv = xpad_ref[n, SUBLANES:SUBLANES + C, :] * cw_ref[CONV_W - 1:CONV_W, :]
        for j in range(CONV_W - 1):
            off = SUBLANES - (CONV_W - 1) + j
            conv = conv + xpad_ref[n, off:off + C, :] * cw_ref[j:j + 1, :]
        xpad_ref[n, 0:SUBLANES, :] = xpad_ref[n, C:C + SUBLANES, :]
        qkv = _silu(conv)
        ab = ab_ref[n]
        g_all = neg_a * jax.nn.softplus(ab + dt_b)
        beta_all = jax.nn.sigmoid(ab)
        if padded:
            g_all = jnp.where(valid, g_all, 0.0)
            beta_all = jnp.where(valid, beta_all, 0.0)
        G_all = _cumsum_rows(ltri, g_all)
        for h in range(N_HEADS):
            gdn_tasks.append(gdn_task(n, h, qkv, G_all, beta_all))
            other_tasks.append(hg_task(n, h))
            other_tasks.append(ret_task(n, h))
    tasks = gdn_tasks + other_tasks
    while tasks:
        alive = []
        for t in tasks:
            try:
                next(t)
                alive.append(t)
            except StopIteration:
                pass
        tasks = alive


def _mixer_call(proj3, ab3, states, conv_w, head_params, gdn_norm, hg_norm, hgrn_lb,
                cosf, sinf, dtab, inner, toend, cdec, *, C, Lv, nb, layer, passes):
    B, Lp, _ = proj3.shape
    n_chunks = Lp // C
    depth = hgrn_lb.shape[0]
    in_state_spec = pl.BlockSpec((None, nb, N_HEADS, HEAD, HEAD), lambda b, i: (layer, b, 0, 0, 0))
    out_state_spec = pl.BlockSpec((nb, N_HEADS, HEAD, HEAD), lambda b, i: (b, 0, 0, 0))
    state_shape = jax.ShapeDtypeStruct((B, N_HEADS, HEAD, HEAD), F32)
    ob_dtype = BF16 if C % (2 * SUBLANES) == 0 else F32

    def whole(shape):
        return pl.BlockSpec(shape, lambda b, i: (0,) * len(shape))

    has_state = states is not None
    operands = [proj3, ab3]
    in_specs = [
        pl.BlockSpec((nb, C, MIXER_COLS), lambda b, i: (b, i, 0)),
        pl.BlockSpec((nb, C, AB_COLS), lambda b, i: (b, i, 0)),
    ]
    if has_state:
        operands += [states[1], states[0], states[2], states[3]]
        in_specs += [pl.BlockSpec((None, nb, CONV_W - 1, CONV_DIM), lambda b, i: (layer, b, 0, 0)),
                     in_state_spec, in_state_spec, in_state_spec]
    operands += [conv_w, head_params, gdn_norm.reshape(1, HEAD), hg_norm.reshape(1, HEAD), hgrn_lb,
                 cosf, sinf, dtab, inner, toend, cdec]

    kern = functools.partial(_mixer_kernel, C=C, Lv=Lv, nb=nb, layer=layer, passes=passes, has_state=has_state)
    return pl.pallas_call(
        kern,
        grid=(B // nb, n_chunks),
        in_specs=in_specs + [
            whole((CONV_W, CONV_DIM)),
            whole((2, AB_COLS)),
            whole((1, HEAD)),
            whole((1, HEAD)),
            whole((depth, MIX)),
            pl.BlockSpec((C, HEAD), lambda b, i: (i, 0)),
            pl.BlockSpec((C, HEAD), lambda b, i: (i, 0)),
            whole((N_HEADS, C, C)),
            whole((N_HEADS, C, HEAD)),
            whole((N_HEADS, C, HEAD)),
            whole((N_HEADS, 1, HEAD)),
        ],
        out_specs=[
            pl.BlockSpec((nb, C, N_BRANCH * MIX), lambda b, i: (b, i, 0)),
            out_state_spec, out_state_spec, out_state_spec,
        ],
        out_shape=[
            jax.ShapeDtypeStruct((B, Lp, N_BRANCH * MIX), ob_dtype),
            state_shape, state_shape, state_shape,
        ],
        scratch_shapes=[
            pltpu.VMEM((nb, C + SUBLANES, CONV_DIM), F32),
            pltpu.VMEM((nb, N_HEADS, 2, C, HEAD), F32),
        ],
        compiler_params=pltpu.CompilerParams(
            dimension_semantics=("arbitrary", "arbitrary"), vmem_limit_bytes=VMEM_LIMIT),
        name="mixer",
    )(*operands)


def _post_kernel(ob_ref, mg_ref, x_ref, gt_ref, wb_ref, wo_ref, o_ref):
    merged = None
    for n in range(N_BRANCH):
        br = _dot(ob_ref[:, n * MIX:(n + 1) * MIX], wb_ref[n])
        term = jax.nn.sigmoid(mg_ref[:, n * D_MODEL:(n + 1) * D_MODEL]) * br
        merged = term if merged is None else merged + term
    o_ref[...] = x_ref[...] + gt_ref[...] * _dot(merged, wo_ref[...])


def _post_call(ob2, proj, x2, mod, per_token, seq_len, w_branch, w_out):
    T = x2.shape[0]
    tm = _row_tile(T if per_token else seq_len, 512)
    tps = max(seq_len // tm, 1)
    merge_blk = MERGE_OFF // (N_BRANCH * D_MODEL)
    return pl.pallas_call(
        _post_kernel,
        grid=(T // tm,),
        in_specs=[
            pl.BlockSpec((tm, N_BRANCH * MIX), lambda i: (i, 0)),
            pl.BlockSpec((tm, N_BRANCH * D_MODEL), lambda i: (i, merge_blk)),
            pl.BlockSpec((tm, D_MODEL), lambda i: (i, 0)),
            _mod_spec(per_token, tm, tps, 2, 1),
            pl.BlockSpec((N_BRANCH, MIX, D_MODEL), lambda i: (0, 0, 0)),
            pl.BlockSpec((D_MODEL, D_MODEL), lambda i: (0, 0)),
        ],
        out_specs=pl.BlockSpec((tm, D_MODEL), lambda i: (i, 0)),
        out_shape=jax.ShapeDtypeStruct((T, D_MODEL), F32),
        compiler_params=pltpu.CompilerParams(
            dimension_semantics=("arbitrary",), vmem_limit_bytes=VMEM_LIMIT),
        name="post",
    )(ob2, proj, x2, mod, w_branch, w_out)


def _ffn_kernel(x_ref, sh_ref, sc_ref, gt_ref, nw_ref, fw_ref, wg_ref, wu_ref, wd_ref, o_ref,
                h_scr, acc_scr, *, final):
    j = pl.program_id(1)

    @pl.when(j == 0)
    def _():
        h = _rms(x_ref[...]) * nw_ref[...]
        h_scr[...] = (h * (1.0 + sc_ref[...]) + sh_ref[...]).astype(BF16)
        acc_scr[...] = jnp.zeros_like(acc_scr)

    h = h_scr[...]
    g = jnp.dot(h, wg_ref[...], preferred_element_type=F32)
    u = jnp.dot(h, wu_ref[...], preferred_element_type=F32)
    acc_scr[...] += _dot(_silu(g) * u, wd_ref[...])

    @pl.when(j == pl.num_programs(1) - 1)
    def _():
        xn = x_ref[...] + gt_ref[...] * acc_scr[...]
        if final:
            xn = _rms(xn) * fw_ref[...]
        o_ref[...] = xn


def _ffn_call(x2, mod, per_token, seq_len, norm_w, final_w, w_gate_up, w_down, final):
    T = x2.shape[0]
    tm = _row_tile(T if per_token else seq_len, 512)
    tf = D_FF // 2
    tps = max(seq_len // tm, 1)
    n_ff = D_FF // tf
    return pl.pallas_call(
        functools.partial(_ffn_kernel, final=final),
        grid=(T // tm, n_ff),
        in_specs=[
            pl.BlockSpec((tm, D_MODEL), lambda i, j: (i, 0)),
            _mod_spec(per_token, tm, tps, 3, 2),
            _mod_spec(per_token, tm, tps, 4, 2),
            _mod_spec(per_token, tm, tps, 5, 2),
            pl.BlockSpec((1, D_MODEL), lambda i, j: (0, 0)),
            pl.BlockSpec((1, D_MODEL), lambda i, j: (0, 0)),
            pl.BlockSpec((D_MODEL, tf), lambda i, j: (0, j)),
            pl.BlockSpec((D_MODEL, tf), lambda i, j: (0, n_ff + j)),
            pl.BlockSpec((tf, D_MODEL), lambda i, j: (j, 0)),
        ],
        out_specs=pl.BlockSpec((tm, D_MODEL), lambda i, j: (i, 0)),
        out_shape=jax.ShapeDtypeStruct((T, D_MODEL), F32),
        scratch_shapes=[pltpu.VMEM((tm, D_MODEL), BF16), pltpu.VMEM((tm, D_MODEL), F32)],
        compiler_params=pltpu.CompilerParams(
            dimension_semantics=("arbitrary", "arbitrary"), vmem_limit_bytes=VMEM_LIMIT),
        name="ffn",
    )(x2, mod, mod, mod, norm_w.reshape(1, D_MODEL), final_w.reshape(1, D_MODEL),
      w_gate_up, w_gate_up, w_down)


def _rope_tables(pos0, length, padded_len):
    half = HEAD // 2
    pos = pos0 + jnp.arange(padded_len, dtype=jnp.int32)
    inv = ROPE_BASE ** (-jnp.arange(half, dtype=F32) / half)
    ang = pos.astype(F32)[:, None] * inv[None, :]
    cos, sin = jnp.cos(ang), jnp.sin(ang)
    return jnp.concatenate([cos, cos], axis=-1), jnp.concatenate([-sin, sin], axis=-1)


def _retention_tables(c_real, c_pad):
    log_gamma = jnp.log1p(-jnp.exp2(-5.0 - jnp.arange(N_HEADS, dtype=F32)))
    idx = jnp.arange(c_real, dtype=F32)
    rel = idx[:, None] - idx[None, :]
    mask = rel >= 0
    lg = log_gamma[:, None, None]
    dtab = jnp.where(mask, jnp.exp(jnp.where(mask, lg * rel, 0.0)), 0.0)
    inner = jnp.exp(log_gamma[:, None] * (idx + 1.0))
    to_end = jnp.exp(log_gamma[:, None] * (c_real - 1.0 - idx))
    cdec = jnp.exp(log_gamma * c_real)
    p = c_pad - c_real
    dtab = jnp.pad(dtab, ((0, 0), (0, p), (0, p)))
    inner = jnp.broadcast_to(jnp.pad(inner, ((0, 0), (0, p)))[:, :, None], (N_HEADS, c_pad, HEAD))
    to_end = jnp.broadcast_to(jnp.pad(to_end, ((0, 0), (0, p)))[:, :, None], (N_HEADS, c_pad, HEAD))
    cdec = jnp.broadcast_to(cdec[:, None, None], (N_HEADS, 1, HEAD))
    return dtab, inner, to_end, cdec


def _trunk(x, mod_all, states, pos0, wts, nb, passes):
    B, L, _ = x.shape
    depth = mod_all.shape[0]
    c_real = CHUNK if L % CHUNK == 0 else L
    assert L % c_real == 0 and c_real <= CHUNK
    C = -(-c_real // SUBLANES) * SUBLANES
    Lp = L if C == c_real else C
    assert Lp == L or L == c_real
    per_token = Lp < 256
    if Lp != L:
        x = jnp.pad(x, ((0, 0), (0, Lp - L), (0, 0)))
    T = B * Lp
    x2 = x.reshape(T, D_MODEL)
    cosf, sinf = _rope_tables(pos0, L, Lp)
    dtab, inner, toend, cdec = _retention_tables(c_real, C)

    outs = ([], [], [], [])
    for l in range(depth):
        if per_token:
            mod = jnp.repeat(mod_all[l], Lp, axis=0)
        else:
            mod = mod_all[l].reshape(B, 1, 6 * D_MODEL)
        proj, ab = _in_call(x2, mod, per_token, Lp, wts['norm_mix'][l], wts['w_main'][l], wts['w_ab'][l])
        proj3 = proj.reshape(B, Lp, MAIN_COLS)
        ob, sg, sr, sh = _mixer_call(
            proj3, ab.reshape(B, Lp, AB_COLS), states,
            wts['conv_w'][l], wts['head_params'][l], wts['gdn_norm'][l], wts['hgrn_norm'][l], wts['hgrn_lb'],
            cosf, sinf, dtab, inner, toend, cdec, C=C, Lv=c_real, nb=nb, layer=l, passes=passes)
        tail = proj3[:, L - min(L, CONV_W - 1):L, GDN_OFF:GDN_OFF + CONV_DIM]
        if L < CONV_W - 1:
            tail = jnp.concatenate([states[1][l][:, L:], tail], axis=1)
        for acc, new in zip(outs, (sg, tail, sr, sh)):
            acc.append(new)
        x1 = _post_call(ob.reshape(T, N_BRANCH * MIX), proj, x2, mod, per_token, Lp,
                        wts['w_branch'][l], wts['w_out'][l])
        x2 = _ffn_call(x1, mod, per_token, Lp, wts['norm_ffn'][l], wts['final_norm'],
                       wts['w_gate_up'][l], wts['w_down'][l], final=(l == depth - 1))
    y = x2.reshape(B, Lp, D_MODEL)[:, :L]
    return y, tuple(jnp.stack(o) for o in outs)


def _prep_weights(w_in, conv_w, gdn_a_log, gdn_dt_bias, gdn_norm, hgrn_lb, hgrn_norm, w_branch, w_out,
                  norm_mix, norm_ffn, w_gate_up, w_down, final_norm):
    ab0 = 4 * MIX
    w_main = jnp.concatenate([w_in[:, :, :ab0], w_in[:, :, ab0 + 2 * N_HEADS:]], axis=-1).astype(BF16)
    w_ab = jnp.pad(w_in[:, :, ab0:ab0 + 2 * N_HEADS],
                   ((0, 0), (0, 0), (0, AB_COLS - 2 * N_HEADS))).astype(BF16)
    head_params = jnp.pad(jnp.stack([gdn_a_log, gdn_dt_bias], axis=1),
                          ((0, 0), (0, 0), (0, AB_COLS - N_HEADS))).astype(F32)
    return dict(
        w_main=w_main, w_ab=w_ab, conv_w=conv_w, head_params=head_params, gdn_norm=gdn_norm,
        hgrn_norm=hgrn_norm, hgrn_lb=hgrn_lb, w_branch=w_branch.astype(BF16), w_out=w_out.astype(BF16),
        norm_mix=norm_mix, norm_ffn=norm_ffn, w_gate_up=w_gate_up.astype(BF16),
        w_down=w_down.astype(BF16), final_norm=final_norm)


def kernel(x_prompt, x_sample, state_gdn, state_gdn_conv, state_ret, state_hgrn, c_prompt, c_sample,
           w_in, conv_w, gdn_a_log, gdn_dt_bias, gdn_norm, hgrn_lb, hgrn_norm, w_branch, w_out,
           w_ada, b_ada, norm_mix, norm_ffn, w_gate_up, w_down, final_norm):
    Bp = x_prompt.shape[0]
    wts = _prep_weights(w_in, conv_w, gdn_a_log, gdn_dt_bias, gdn_norm, hgrn_lb, hgrn_norm, w_branch, w_out,
                        norm_mix, norm_ffn, w_gate_up, w_down, final_norm)
    c_all = jnp.concatenate([c_prompt, c_sample], axis=0)
    mod_all = _ada_call(c_all, w_ada, b_ada)
    y_p, st_p = _trunk(x_prompt, mod_all[:, :Bp], None, 0, wts, nb=2, passes=1)
    y_s, st_s = _trunk(x_sample, mod_all[:, Bp:], (state_gdn, state_gdn_conv, state_ret, state_hgrn),
                       PAST_LEN, wts, nb=8, passes=1)
    return (y_p, y_s) + st_p + st_s
```

```python
import functools

import jax
import jax.numpy as jnp
from jax import lax
from jax.experimental import pallas as pl
from jax.experimental.pallas import tpu as pltpu

F32 = jnp.float32
BF16 = jnp.bfloat16

D_MODEL = 1024
N_HEADS = 4
HEAD = 128
MIX = N_HEADS * HEAD
CONV_W = 4
CONV_DIM = 3 * MIX
CHUNK = 64
N_BRANCH = 3
D_FF = 2816
ROPE_BASE = 10000.0
EPS = 1e-6
F_TINY = 1e-30
PAST_LEN = 16384

GDN_OFF = 0
RET_OFF = 4 * MIX
HG_OFF = 8 * MIX
MIXER_COLS = 12 * MIX
MERGE_OFF = MIXER_COLS
MAIN_COLS = MIXER_COLS + N_BRANCH * D_MODEL
AB_COLS = 128
IN_TN = 1536
IN_TILES = MAIN_COLS // IN_TN

SUBLANES = 8
VMEM_LIMIT = 56 * 1024 * 1024

_DN = {
    'nn': (((1,), (0,)), ((), ())),
    'nt': (((1,), (1,)), ((), ())),
    'tn': (((0,), (0,)), ((), ())),
}


def _split2(a):
    hi = a.astype(BF16)
    lo = (a - hi.astype(F32)).astype(BF16)
    return hi, lo


def _dot(a, b, dims='nn', passes=1):
    dn = _DN[dims]
    if passes == 1:
        return lax.dot_general(a.astype(BF16), b.astype(BF16), dn, preferred_element_type=F32)
    a_hi, a_lo = _split2(a)
    b_hi, b_lo = _split2(b)
    out = lax.dot_general(a_hi, b_lo, dn, preferred_element_type=F32)
    out = out + lax.dot_general(a_lo, b_hi, dn, preferred_element_type=F32)
    return out + lax.dot_general(a_hi, b_hi, dn, preferred_element_type=F32)


def _cumsum_rows(ltri_bf16, x):
    x1 = x.astype(BF16)
    r1 = x - x1.astype(F32)
    x2 = r1.astype(BF16)
    x3 = (r1 - x2.astype(F32)).astype(BF16)
    dn = _DN['nn']
    out = lax.dot_general(ltri_bf16, x3, dn, preferred_element_type=F32)
    out = out + lax.dot_general(ltri_bf16, x2, dn, preferred_element_type=F32)
    return out + lax.dot_general(ltri_bf16, x1, dn, preferred_element_type=F32)


def _silu(x):
    return x * jax.nn.sigmoid(x)


def _rms(x):
    return x * lax.rsqrt(jnp.mean(x * x, axis=-1, keepdims=True) + EPS)


def _ada_kernel(c_ref, w_ref, b_ref, o_ref):
    cs = _silu(c_ref[...])
    o_ref[...] = _dot(cs, w_ref[...]) + b_ref[...]


def _ada_call(c_all, w_ada, b_ada):
    depth = w_ada.shape[0]
    rows = c_all.shape[0]
    n_out = w_ada.shape[2]
    tn = 1536
    return pl.pallas_call(
        _ada_kernel,
        grid=(depth, n_out // tn),
        in_specs=[
            pl.BlockSpec((rows, D_MODEL), lambda l, j: (0, 0)),
            pl.BlockSpec((None, D_MODEL, tn), lambda l, j: (l, 0, j)),
            pl.BlockSpec((None, 1, tn), lambda l, j: (l, 0, j)),
        ],
        out_specs=pl.BlockSpec((None, rows, tn), lambda l, j: (l, 0, j)),
        out_shape=jax.ShapeDtypeStruct((depth, rows, n_out), F32),
        compiler_params=pltpu.CompilerParams(
            dimension_semantics=("arbitrary", "arbitrary"), vmem_limit_bytes=VMEM_LIMIT),
        name="ada",
    )(c_all, w_ada, b_ada.reshape(depth, 1, n_out))


def _row_tile(n_rows, preferred):
    tm = preferred
    while n_rows % tm:
        tm //= 2
    return tm


def _mod_spec(per_token, tm, tiles_per_seq, seg, ngrid):
    if per_token:
        if ngrid == 2:
            return pl.BlockSpec((tm, D_MODEL), lambda i, j: (i, seg))
        return pl.BlockSpec((tm, D_MODEL), lambda i: (i, seg))
    if ngrid == 2:
        return pl.BlockSpec((None, 1, D_MODEL), lambda i, j: (i // tiles_per_seq, 0, seg))
    return pl.BlockSpec((None, 1, D_MODEL), lambda i: (i // tiles_per_seq, 0, seg))


def _in_kernel(x_ref, sh_ref, sc_ref, nw_ref, w_ref, wab_ref, o_ref, ab_ref, h_scr):
    @pl.when(pl.program_id(1) == 0)
    def _():
        h = _rms(x_ref[...]) * nw_ref[...]
        h = h * (1.0 + sc_ref[...]) + sh_ref[...]
        hb = h.astype(BF16)
        h_scr[...] = hb
        ab_ref[...] = jnp.dot(hb, wab_ref[...], preferred_element_type=F32)

    o_ref[...] = jnp.dot(h_scr[...], w_ref[pl.program_id(1)], preferred_element_type=F32)


def _in_call(x2, mod, per_token, seq_len, norm_w, w_main, w_ab):
    T = x2.shape[0]
    tm = _row_tile(T if per_token else seq_len, 512 if per_token else 1024)
    tn = IN_TN
    tps = max(seq_len // tm, 1)
    return pl.pallas_call(
        _in_kernel,
        grid=(T // tm, IN_TILES),
        in_specs=[
            pl.BlockSpec((tm, D_MODEL), lambda i, j: (i, 0)),
            _mod_spec(per_token, tm, tps, 0, 2),
            _mod_spec(per_token, tm, tps, 1, 2),
            pl.BlockSpec((1, D_MODEL), lambda i, j: (0, 0)),
            pl.BlockSpec((IN_TILES, D_MODEL, tn), lambda i, j: (0, 0, 0), pipeline_mode=pl.Buffered(1)),
            pl.BlockSpec((D_MODEL, AB_COLS), lambda i, j: (0, 0)),
        ],
        out_specs=[
            pl.BlockSpec((tm, tn), lambda i, j: (i, j)),
            pl.BlockSpec((tm, AB_COLS), lambda i, j: (i, 0)),
        ],
        out_shape=[
            jax.ShapeDtypeStruct((T, MAIN_COLS), F32),
            jax.ShapeDtypeStruct((T, AB_COLS), F32),
        ],
        scratch_shapes=[pltpu.VMEM((tm, D_MODEL), BF16)],
        compiler_params=pltpu.CompilerParams(
            dimension_semantics=("arbitrary", "arbitrary"), vmem_limit_bytes=VMEM_LIMIT),
        name="in_proj",
    )(x2, mod, mod, norm_w.reshape(1, D_MODEL), w_main, w_ab)


def _mixer_kernel(*refs, C, Lv, nb, layer, passes, has_state):
    proj_ref, ab_ref = refs[:2]
    pos = 2
    if has_state:
        conv0_ref, sg0_ref, sr0_ref, sh0_ref = refs[pos:pos + 4]
        pos += 4
    (cw_ref, hp_ref, gn_ref, hn_ref, lb_ref, cos_ref, sin_ref,
     dtab_ref, inner_ref, toend_ref, cdec_ref) = refs[pos:pos + 11]
    ob_ref, sg_ref, sr_ref, sh_ref, xpad_ref, hg_ref = refs[pos + 11:]
    tail0 = SUBLANES - (CONV_W - 1)

    @pl.when(pl.program_id(1) == 0)
    def _():
        if has_state:
            sg_ref[...] = sg0_ref[...]
            sr_ref[...] = sr0_ref[...]
            sh_ref[...] = sh0_ref[...]
            xpad_ref[:, tail0:SUBLANES, :] = conv0_ref[...]
        else:
            sg_ref[...] = jnp.zeros_like(sg_ref)
            sr_ref[...] = jnp.zeros_like(sr_ref)
            sh_ref[...] = jnp.zeros_like(sh_ref)
            xpad_ref[:, tail0:SUBLANES, :] = jnp.zeros((nb, CONV_W - 1, CONV_DIM), F32)

    padded = Lv < C
    row1 = lax.broadcasted_iota(jnp.int32, (C, 1), 0)
    valid = row1 < Lv
    ri = lax.broadcasted_iota(jnp.int32, (C, C), 0)
    ci = lax.broadcasted_iota(jnp.int32, (C, C), 1)
    incl = ri >= ci
    strict = ri > ci
    eye = ri == ci
    ltri = jnp.where(incl, 1.0, 0.0).astype(BF16)
    r128 = lax.broadcasted_iota(jnp.int32, (HEAD, HEAD), 0)
    c128 = lax.broadcasted_iota(jnp.int32, (HEAD, HEAD), 1)
    eye128 = r128 == c128
    n_levels = max((Lv - 1).bit_length(), 1)

    lbp = lb_ref[...]
    lbe = jnp.exp(lbp - jnp.max(lbp, axis=0, keepdims=True))
    lbw = lbe / jnp.sum(lbe, axis=0, keepdims=True)
    lb_row = lbw[0:1, :]
    for d in range(1, layer + 1):
        lb_row = lb_row + lbw[d:d + 1, :]
    lb_row = lb_row - lbw[0:1, :]

    neg_a = -jnp.exp(hp_ref[0:1, :])
    dt_b = hp_ref[1:2, :]
    gn = gn_ref[...]
    hn = hn_ref[...]
    cosf = cos_ref[...]
    sinf = sin_ref[...]

    SB = SUBLANES

    def gdn_task(n, h, qkv, G_all, beta_all):
        q = qkv[:, h * HEAD:(h + 1) * HEAD]
        k = qkv[:, MIX + h * HEAD:MIX + (h + 1) * HEAD]
        v = qkv[:, 2 * MIX + h * HEAD:2 * MIX + (h + 1) * HEAD]
        q = q * lax.rsqrt(jnp.sum(q * q, axis=-1, keepdims=True) + EPS) * (HEAD ** -0.5)
        k = k * lax.rsqrt(jnp.sum(k * k, axis=-1, keepdims=True) + EPS)
        kk = _dot(k, k, 'nt', passes)
        qk = _dot(q, k, 'nt', passes)
        yield
        Gc = G_all[:, h:h + 1]
        beta = beta_all[:, N_HEADS + h:N_HEADS + h + 1]
        Gr = jnp.sum(jnp.where(eye, Gc, 0.0), axis=0, keepdims=True)
        G_last = G_all[C - 1:C, h:h + 1]
        eG = jnp.exp(Gc)
        decay = jnp.where(incl, jnp.exp(jnp.where(incl, Gc - Gr, 0.0)), 0.0)
        P = -jnp.where(strict, beta * kk * decay, 0.0)
        sol = jnp.concatenate([beta * v, (beta * eG) * k], axis=-1)
        for lvl in range(n_levels):
            upd = _dot(P, sol, 'nn', passes)
            if lvl + 1 < n_levels:
                P = _dot(P, P, 'nn', passes)
            yield
            sol = sol + upd
        S = sg_ref[n, h]
        sk_s = _dot(sol[:, HEAD:], S, 'nn', passes)
        q_s = _dot(q, S, 'nn', passes)
        yield
        u = sol[:, :HEAD] - sk_s
        k_end = k * jnp.exp(G_last - Gc)
        intra = _dot(qk * decay, u, 'nn', passes)
        s_upd = _dot(k_end, u, 'tn', passes)
        yield
        o = eG * q_s + intra
        sg_ref[n, h] = jnp.exp(G_last) * S + s_upd
        z = proj_ref[n, :, GDN_OFF + 3 * MIX + h * HEAD:GDN_OFF + 3 * MIX + (h + 1) * HEAD]
        ob_ref[n, :, h * HEAD:(h + 1) * HEAD] = ((_rms(o) * gn) * _silu(z)).astype(ob_ref.dtype)

    def ret_task(n, h):
        base = RET_OFF + h * HEAD
        q = proj_ref[n, :, base:base + HEAD]
        k = proj_ref[n, :, base + MIX:base + MIX + HEAD]
        v = proj_ref[n, :, base + 2 * MIX:base + 2 * MIX + HEAD]
        q = q * cosf + pltpu.roll(q, HEAD // 2, 1) * sinf
        k = (k * cosf + pltpu.roll(k, HEAD // 2, 1) * sinf) * (HEAD ** -0.5)
        if padded:
            v = jnp.where(valid, v, 0.0)
        S = sr_ref[n, h]
        qk = _dot(q, k, 'nt', passes)
        q_s = _dot(q, S, 'nn', passes)
        s_upd = _dot(k * toend_ref[h], v, 'tn', passes)
        yield
        intra = _dot(qk * dtab_ref[h], v, 'nn', passes)
        sr_ref[n, h] = cdec_ref[h] * S + s_upd
        yield
        o = intra + inner_ref[h] * q_s
        gate = proj_ref[n, :, base + 3 * MIX:base + 3 * MIX + HEAD]
        mu = jnp.mean(o, axis=-1, keepdims=True)
        oc = o - mu
        var = jnp.mean(oc * oc, axis=-1, keepdims=True)
        ob_ref[n, :, MIX + h * HEAD:MIX + (h + 1) * HEAD] = (
            oc * lax.rsqrt(var + EPS) * _silu(gate)).astype(ob_ref.dtype)

    def hg_task(n, h):
        base = HG_OFF + h * HEAD
        qc = proj_ref[n, :, base:base + HEAD]
        zf = proj_ref[n, :, base + MIX:base + MIX + HEAD]
        vc = proj_ref[n, :, base + 2 * MIX:base + 2 * MIX + HEAD]
        lb_h = lb_row[:, h * HEAD:(h + 1) * HEAD]
        s_neg = jax.nn.sigmoid(-zf)
        f_gate = jax.nn.sigmoid(zf) + lb_h * s_neg
        log_f = jnp.log(jnp.maximum(f_gate, F_TINY))
        kc = (1.0 - lb_h) * s_neg
        if padded:
            log_f = jnp.where(valid, log_f, 0.0)
            kc = jnp.where(valid, kc, 0.0)
            vc = jnp.where(valid, vc, 0.0)
        A = _cumsum_rows(ltri, log_f)
        yield
        hgs = hg_ref.at[n, h]
        hgs[0] = A
        hgs[1] = kc
        S = sh_ref[n, h]
        a_last = hgs[0, C - 1:C, :]
        q_s = _dot(qc * jnp.exp(A), S, 'nn', passes)
        s_upd = _dot(kc * jnp.exp(a_last - A), vc, 'tn', passes)
        score_rows = []
        for blk in range(C // SB):
            r0 = blk * SB
            qI = qc[r0:r0 + SB]
            AI = A[r0:r0 + SB]
            rowg = r0 + lax.broadcasted_iota(jnp.int32, (SB, C), 0)
            colg = lax.broadcasted_iota(jnp.int32, (SB, C), 1)
            cross = None
            if r0 > 0:
                a_b = hgs[0, r0 - 1:r0, :]
                kt = jnp.concatenate([kc[:r0] * jnp.exp(a_b - A[:r0]), jnp.zeros((C - r0, HEAD), F32)], axis=0)
                qt = qI * jnp.exp(AI - a_b)
                cross = _dot(qt, kt, 'nt', passes)
            sc_blk = jnp.zeros((SB, C), F32)
            for s in range(SB):
                a_s = hgs[0, r0 + s:r0 + s + 1, :]
                k_s = hgs[1, r0 + s:r0 + s + 1, :]
                w = jnp.exp(AI - a_s)
                val = jnp.sum(qI * k_s * w, axis=-1, keepdims=True)
                sc_blk = jnp.where(colg == r0 + s, val, sc_blk)
            yield
            if cross is not None:
                sc_blk = sc_blk + cross
            score_rows.append(jnp.where(rowg >= colg, sc_blk, 0.0))
        scores = score_rows[0] if len(score_rows) == 1 else jnp.concatenate(score_rows, axis=0)
        intra = _dot(scores, vc, 'nn', passes)
        dec_col = jnp.sum(jnp.where(eye128, jnp.exp(a_last), 0.0), axis=1, keepdims=True)
        sh_ref[n, h] = dec_col * S + s_upd
        yield
        o = intra + q_s
        gate = proj_ref[n, :, base + 3 * MIX:base + 3 * MIX + HEAD]
        ob_ref[n, :, 2 * MIX + h * HEAD:2 * MIX + (h + 1) * HEAD] = (
            (_rms(o) * hn) * jax.nn.sigmoid(gate)).astype(ob_ref.dtype)

    gdn_tasks, other_tasks = [], []
    for n in range(nb):
        xpad_ref[n, SUBLANES:SUBLANES + C, :] = proj_ref[n, :, GDN_OFF:GDN_OFF + CONV_DIM]
        conv = xpad_ref[n, SUBLANES:SUBLANES + C, :] * cw_ref[CONV_W - 1:CONV_W, :]
        for j in range(CONV_W - 1):
            off = SUBLANES - (CONV_W - 1) + j
            conv = conv + xpad_ref[n, off:off + C, :] * cw_ref[j:j + 1, :]
        xpad_ref[n, 0:SUBLANES, :] = xpad_ref[n, C:C + SUBLANES, :]
        qkv = _silu(conv)
        ab = ab_ref[n]
        g_all = neg_a * jax.nn.softplus(ab + dt_b)
        beta_all = jax.nn.sigmoid(ab)
        if padded:
            g_all = jnp.where(valid, g_all, 0.0)
            beta_all = jnp.where(valid, beta_all, 0.0)
        G_all = _cumsum_rows(ltri, g_all)
        for h in range(N_HEADS):
            gdn_tasks.append(gdn_task(n, h, qkv, G_all, beta_all))
            other_tasks.append(hg_task(n, h))
            other_tasks.append(ret_task(n, h))
    tasks = gdn_tasks + other_tasks
    while tasks:
        alive = []
        for t in tasks:
            try:
                next(t)
                alive.append(t)
            except StopIteration:
                pass
        tasks = alive


def _mixer_call(proj3, ab3, states, conv_w, head_params, gdn_norm, hg_norm, hgrn_lb,
                cosf, sinf, dtab, inner, toend, cdec, *, C, Lv, nb, layer, passes):
    B, Lp, _ = proj3.shape
    n_chunks = Lp // C
    depth = hgrn_lb.shape[0]
    in_state_spec = pl.BlockSpec((None, nb, N_HEADS, HEAD, HEAD), lambda b, i: (layer, b, 0, 0, 0))
    out_state_spec = pl.BlockSpec((nb, N_HEADS, HEAD, HEAD), lambda b, i: (b, 0, 0, 0))
    state_shape = jax.ShapeDtypeStruct((B, N_HEADS, HEAD, HEAD), F32)
    ob_dtype = BF16 if C % (2 * SUBLANES) == 0 else F32

    def whole(shape):
        return pl.BlockSpec(shape, lambda b, i: (0,) * len(shape))

    has_state = states is not None
    operands = [proj3, ab3]
    in_specs = [
        pl.BlockSpec((nb, C, MIXER_COLS), lambda b, i: (b, i, 0)),
        pl.BlockSpec((nb, C, AB_COLS), lambda b, i: (b, i, 0)),
    ]
    if has_state:
        operands += [states[1], states[0], states[2], states[3]]
        in_specs += [pl.BlockSpec((None, nb, CONV_W - 1, CONV_DIM), lambda b, i: (layer, b, 0, 0)),
                     in_state_spec, in_state_spec, in_state_spec]
    operands += [conv_w, head_params, gdn_norm.reshape(1, HEAD), hg_norm.reshape(1, HEAD), hgrn_lb,
                 cosf, sinf, dtab, inner, toend, cdec]

    kern = functools.partial(_mixer_kernel, C=C, Lv=Lv, nb=nb, layer=layer, passes=passes, has_state=has_state)
    return pl.pallas_call(
        kern,
        grid=(B // nb, n_chunks),
        in_specs=in_specs + [
            whole((CONV_W, CONV_DIM)),
            whole((2, AB_COLS)),
            whole((1, HEAD)),
            whole((1, HEAD)),
            whole((depth, MIX)),
            pl.BlockSpec((C, HEAD), lambda b, i: (i, 0)),
            pl.BlockSpec((C, HEAD), lambda b, i: (i, 0)),
            whole((N_HEADS, C, C)),
            whole((N_HEADS, C, HEAD)),
            whole((N_HEADS, C, HEAD)),
            whole((N_HEADS, 1, HEAD)),
        ],
        out_specs=[
            pl.BlockSpec((nb, C, N_BRANCH * MIX), lambda b, i: (b, i, 0)),
            out_state_spec, out_state_spec, out_state_spec,
        ],
        out_shape=[
            jax.ShapeDtypeStruct((B, Lp, N_BRANCH * MIX), ob_dtype),
            state_shape, state_shape, state_shape,
        ],
        scratch_shapes=[
            pltpu.VMEM((nb, C + SUBLANES, CONV_DIM), F32),
            pltpu.VMEM((nb, N_HEADS, 2, C, HEAD), F32),
        ],
        compiler_params=pltpu.CompilerParams(
            dimension_semantics=("arbitrary", "arbitrary"), vmem_limit_bytes=VMEM_LIMIT),
        name="mixer",
    )(*operands)


def _post_kernel(ob_ref, mg_ref, x_ref, gt_ref, wb_ref, wo_ref, o_ref):
    merged = None
    for n in range(N_BRANCH):
        br = _dot(ob_ref[:, n * MIX:(n + 1) * MIX], wb_ref[n])
        term = jax.nn.sigmoid(mg_ref[:, n * D_MODEL:(n + 1) * D_MODEL]) * br
        merged = term if merged is None else merged + term
    o_ref[...] = x_ref[...] + gt_ref[...] * _dot(merged, wo_ref[...])


def _post_call(ob2, proj, x2, mod, per_token, seq_len, w_branch, w_out):
    T = x2.shape[0]
    tm = _row_tile(T if per_token else seq_len, 512)
    tps = max(seq_len // tm, 1)
    merge_blk = MERGE_OFF // (N_BRANCH * D_MODEL)
    return pl.pallas_call(
        _post_kernel,
        grid=(T // tm,),
        in_specs=[
            pl.BlockSpec((tm, N_BRANCH * MIX), lambda i: (i, 0)),
            pl.BlockSpec((tm, N_BRANCH * D_MODEL), lambda i: (i, merge_blk)),
            pl.BlockSpec((tm, D_MODEL), lambda i: (i, 0)),
            _mod_spec(per_token, tm, tps, 2, 1),
            pl.BlockSpec((N_BRANCH, MIX, D_MODEL), lambda i: (0, 0, 0)),
            pl.BlockSpec((D_MODEL, D_MODEL), lambda i: (0, 0)),
        ],
        out_specs=pl.BlockSpec((tm, D_MODEL), lambda i: (i, 0)),
        out_shape=jax.ShapeDtypeStruct((T, D_MODEL), F32),
        compiler_params=pltpu.CompilerParams(
            dimension_semantics=("arbitrary",), vmem_limit_bytes=VMEM_LIMIT),
        name="post",
    )(ob2, proj, x2, mod, w_branch, w_out)


FF_SPLITS = (0, 1024, 2048, D_FF)


def _ffn_kernel(x_ref, sh_ref, sc_ref, gt_ref, nw_ref, fw_ref, wg_ref, wu_ref, wd_ref, o_ref, *, final):
    x = x_ref[...]
    h = (_rms(x) * nw_ref[...] * (1.0 + sc_ref[...]) + sh_ref[...]).astype(BF16)
    y = None
    for lo, hi in zip(FF_SPLITS[:-1], FF_SPLITS[1:]):
        g = jnp.dot(h, wg_ref[:, lo:hi], preferred_element_type=F32)
        u = jnp.dot(h, wu_ref[:, lo:hi], preferred_element_type=F32)
        part = _dot(_silu(g) * u, wd_ref[lo:hi, :])
        y = part if y is None else y + part
    xn = x + gt_ref[...] * y
    if final:
        xn = _rms(xn) * fw_ref[...]
    o_ref[...] = xn


def _ffn_call(x2, mod, per_token, seq_len, norm_w, final_w, w_gate_up, w_down, final):
    T = x2.shape[0]
    tm = _row_tile(T if per_token else seq_len, 512)
    tps = max(seq_len // tm, 1)
    resident = pl.Buffered(1)
    return pl.pallas_call(
        functools.partial(_ffn_kernel, final=final),
        grid=(T // tm,),
        in_specs=[
            pl.BlockSpec((tm, D_MODEL), lambda i: (i, 0)),
            _mod_spec(per_token, tm, tps, 3, 1),
            _mod_spec(per_token, tm, tps, 4, 1),
            _mod_spec(per_token, tm, tps, 5, 1),
            pl.BlockSpec((1, D_MODEL), lambda i: (0, 0)),
            pl.BlockSpec((1, D_MODEL), lambda i: (0, 0)),
            pl.BlockSpec((D_MODEL, D_FF), lambda i: (0, 0), pipeline_mode=resident),
            pl.BlockSpec((D_MODEL, D_FF), lambda i: (0, 1), pipeline_mode=resident),
            pl.BlockSpec((D_FF, D_MODEL), lambda i: (0, 0), pipeline_mode=resident),
        ],
        out_specs=pl.BlockSpec((tm, D_MODEL), lambda i: (i, 0)),
        out_shape=jax.ShapeDtypeStruct((T, D_MODEL), F32),
        compiler_params=pltpu.CompilerParams(
            dimension_semantics=("arbitrary",), vmem_limit_bytes=VMEM_LIMIT),
        name="ffn",
    )(x2, mod, mod, mod, norm_w.reshape(1, D_MODEL), final_w.reshape(1, D_MODEL),
      w_gate_up, w_gate_up, w_down)


def _rope_tables(pos0, length, padded_len):
    half = HEAD // 2
    pos = pos0 + jnp.arange(padded_len, dtype=jnp.int32)
    inv = ROPE_BASE ** (-jnp.arange(half, dtype=F32) / half)
    ang = pos.astype(F32)[:, None] * inv[None, :]
    cos, sin = jnp.cos(ang), jnp.sin(ang)
    return jnp.concatenate([cos, cos], axis=-1), jnp.concatenate([-sin, sin], axis=-1)


def _retention_tables(c_real, c_pad):
    log_gamma = jnp.log1p(-jnp.exp2(-5.0 - jnp.arange(N_HEADS, dtype=F32)))
    idx = jnp.arange(c_real, dtype=F32)
    rel = idx[:, None] - idx[None, :]
    mask = rel >= 0
    lg = log_gamma[:, None, None]
    dtab = jnp.where(mask, jnp.exp(jnp.where(mask, lg * rel, 0.0)), 0.0)
    inner = jnp.exp(log_gamma[:, None] * (idx + 1.0))
    to_end = jnp.exp(log_gamma[:, None] * (c_real - 1.0 - idx))
    cdec = jnp.exp(log_gamma * c_real)
    p = c_pad - c_real
    dtab = jnp.pad(dtab, ((0, 0), (0, p), (0, p)))
    inner = jnp.broadcast_to(jnp.pad(inner, ((0, 0), (0, p)))[:, :, None], (N_HEADS, c_pad, HEAD))
    to_end = jnp.broadcast_to(jnp.pad(to_end, ((0, 0), (0, p)))[:, :, None], (N_HEADS, c_pad, HEAD))
    cdec = jnp.broadcast_to(cdec[:, None, None], (N_HEADS, 1, HEAD))
    return dtab, inner, to_end, cdec


def _trunk(x, mod_all, states, pos0, wts, nb, passes):
    B, L, _ = x.shape
    depth = mod_all.shape[0]
    c_real = CHUNK if L % CHUNK == 0 else L
    assert L % c_real == 0 and c_real <= CHUNK
    C = -(-c_real // SUBLANES) * SUBLANES
    Lp = L if C == c_real else C
    assert Lp == L or L == c_real
    per_token = Lp < 256
    if Lp != L:
        x = jnp.pad(x, ((0, 0), (0, Lp - L), (0, 0)))
    T = B * Lp
    x2 = x.reshape(T, D_MODEL)
    cosf, sinf = _rope_tables(pos0, L, Lp)
    dtab, inner, toend, cdec = _retention_tables(c_real, C)

    outs = ([], [], [], [])
    for l in range(depth):
        if per_token:
            mod = jnp.repeat(mod_all[l], Lp, axis=0)
        else:
            mod = mod_all[l].reshape(B, 1, 6 * D_MODEL)
        proj, ab = _in_call(x2, mod, per_token, Lp, wts['norm_mix'][l], wts['w_main'][l], wts['w_ab'][l])
        proj3 = proj.reshape(B, Lp, MAIN_COLS)
        ob, sg, sr, sh = _mixer_call(
            proj3, ab.reshape(B, Lp, AB_COLS), states,
            wts['conv_w'][l], wts['head_params'][l], wts['gdn_norm'][l], wts['hgrn_norm'][l], wts['hgrn_lb'],
            cosf, sinf, dtab, inner, toend, cdec, C=C, Lv=c_real, nb=nb, layer=l, passes=passes)
        tail = proj3[:, L - min(L, CONV_W - 1):L, GDN_OFF:GDN_OFF + CONV_DIM]
        if L < CONV_W - 1:
            tail = jnp.concatenate([states[1][l][:, L:], tail], axis=1)
        for acc, new in zip(outs, (sg, tail, sr, sh)):
            acc.append(new)
        x1 = _post_call(ob.reshape(T, N_BRANCH * MIX), proj, x2, mod, per_token, Lp,
                        wts['w_branch'][l], wts['w_out'][l])
        x2 = _ffn_call(x1, mod, per_token, Lp, wts['norm_ffn'][l], wts['final_norm'],
                       wts['w_gate_up'][l], wts['w_down'][l], final=(l == depth - 1))
    y = x2.reshape(B, Lp, D_MODEL)[:, :L]
    return y, tuple(jnp.stack(o) for o in outs)


def _prep_weights(w_in, conv_w, gdn_a_log, gdn_dt_bias, gdn_norm, hgrn_lb, hgrn_norm, w_branch, w_out,
                  norm_mix, norm_ffn, w_gate_up, w_down, final_norm):
    ab0 = 4 * MIX
    w_main = jnp.concatenate([w_in[:, :, :ab0], w_in[:, :, ab0 + 2 * N_HEADS:]], axis=-1).astype(BF16)
    w_main = w_main.reshape(-1, D_MODEL, IN_TILES, IN_TN).transpose(0, 2, 1, 3)
    w_ab = jnp.pad(w_in[:, :, ab0:ab0 + 2 * N_HEADS],
                   ((0, 0), (0, 0), (0, AB_COLS - 2 * N_HEADS))).astype(BF16)
    head_params = jnp.pad(jnp.stack([gdn_a_log, gdn_dt_bias], axis=1),
                          ((0, 0), (0, 0), (0, AB_COLS - N_HEADS))).astype(F32)
    return dict(
        w_main=w_main, w_ab=w_ab, conv_w=conv_w, head_params=head_params, gdn_norm=gdn_norm,
        hgrn_norm=hgrn_norm, hgrn_lb=hgrn_lb, w_branch=w_branch.astype(BF16), w_out=w_out.astype(BF16),
        norm_mix=norm_mix, norm_ffn=norm_ffn, w_gate_up=w_gate_up.astype(BF16),
        w_down=w_down.astype(BF16), final_norm=final_norm)


def kernel(x_prompt, x_sample, state_gdn, state_gdn_conv, state_ret, state_hgrn, c_prompt, c_sample,
           w_in, conv_w, gdn_a_log, gdn_dt_bias, gdn_norm, hgrn_lb, hgrn_norm, w_branch, w_out,
           w_ada, b_ada, norm_mix, norm_ffn, w_gate_up, w_down, final_norm):
    Bp = x_prompt.shape[0]
    wts = _prep_weights(w_in, conv_w, gdn_a_log, gdn_dt_bias, gdn_norm, hgrn_lb, hgrn_norm, w_branch, w_out,
                        norm_mix, norm_ffn, w_gate_up, w_down, final_norm)
    c_all = jnp.concatenate([c_prompt, c_sample], axis=0)
    mod_all = _ada_call(c_all, w_ada, b_ada)
    y_p, st_p = _trunk(x_prompt, mod_all[:, :Bp], None, 0, wts, nb=2, passes=1)
    y_s, st_s = _trunk(x_sample, mod_all[:, Bp:], (state_gdn, state_gdn_conv, state_ret, state_hgrn),
                       PAST_LEN, wts, nb=8, passes=1)
    return (y_p, y_s) + st_p + st_s
```

```python
import functools

import jax
import jax.numpy as jnp
from jax import lax
from jax.experimental import pallas as pl
from jax.experimental.pallas import tpu as pltpu

F32 = jnp.float32
BF16 = jnp.bfloat16

D_MODEL = 1024
N_HEADS = 4
HEAD = 128
MIX = N_HEADS * HEAD
CONV_W = 4
CONV_DIM = 3 * MIX
CHUNK = 64
N_BRANCH = 3
D_FF = 2816
ROPE_BASE = 10000.0
EPS = 1e-6
F_TINY = 1e-30
PAST_LEN = 16384

GDN_OFF = 0
RET_OFF = 4 * MIX
HG_OFF = 8 * MIX
MIXER_COLS = 12 * MIX
MERGE_OFF = MIXER_COLS
MAIN_COLS = MIXER_COLS + N_BRANCH * D_MODEL
AB_COLS = 128
IN_TN = 1536
IN_TILES = MAIN_COLS // IN_TN
MIXER_TILES = MIXER_COLS // IN_TN

SUBLANES = 8
VMEM_LIMIT = 56 * 1024 * 1024

_DN = {
    'nn': (((1,), (0,)), ((), ())),
    'nt': (((1,), (1,)), ((), ())),
    'tn': (((0,), (0,)), ((), ())),
}


def _split2(a):
    hi = a.astype(BF16)
    lo = (a - hi.astype(F32)).astype(BF16)
    return hi, lo


def _dot(a, b, dims='nn', passes=1):
    dn = _DN[dims]
    if passes == 1:
        return lax.dot_general(a.astype(BF16), b.astype(BF16), dn, preferred_element_type=F32)
    a_hi, a_lo = _split2(a)
    b_hi, b_lo = _split2(b)
    out = lax.dot_general(a_hi, b_lo, dn, preferred_element_type=F32)
    out = out + lax.dot_general(a_lo, b_hi, dn, preferred_element_type=F32)
    return out + lax.dot_general(a_hi, b_hi, dn, preferred_element_type=F32)


def _cumsum_rows(ltri_bf16, x):
    x1 = x.astype(BF16)
    r1 = x - x1.astype(F32)
    x2 = r1.astype(BF16)
    x3 = (r1 - x2.astype(F32)).astype(BF16)
    dn = _DN['nn']
    out = lax.dot_general(ltri_bf16, x3, dn, preferred_element_type=F32)
    out = out + lax.dot_general(ltri_bf16, x2, dn, preferred_element_type=F32)
    return out + lax.dot_general(ltri_bf16, x1, dn, preferred_element_type=F32)


def _silu(x):
    return x * jax.nn.sigmoid(x)


def _rms(x):
    return x * lax.rsqrt(jnp.mean(x * x, axis=-1, keepdims=True) + EPS)


def _ada_kernel(c_ref, w_ref, b_ref, o_ref):
    cs = _silu(c_ref[...])
    o_ref[...] = _dot(cs, w_ref[...]) + b_ref[...]


def _ada_call(c_all, w_ada, b_ada):
    depth = w_ada.shape[0]
    rows = c_all.shape[0]
    n_out = w_ada.shape[2]
    tn = 1536
    return pl.pallas_call(
        _ada_kernel,
        grid=(depth, n_out // tn),
        in_specs=[
            pl.BlockSpec((rows, D_MODEL), lambda l, j: (0, 0)),
            pl.BlockSpec((None, D_MODEL, tn), lambda l, j: (l, 0, j)),
            pl.BlockSpec((None, 1, tn), lambda l, j: (l, 0, j)),
        ],
        out_specs=pl.BlockSpec((None, rows, tn), lambda l, j: (l, 0, j)),
        out_shape=jax.ShapeDtypeStruct((depth, rows, n_out), F32),
        compiler_params=pltpu.CompilerParams(
            dimension_semantics=("arbitrary", "arbitrary"), vmem_limit_bytes=VMEM_LIMIT),
        name="ada",
    )(c_all, w_ada, b_ada.reshape(depth, 1, n_out))


def _row_tile(n_rows, preferred):
    tm = preferred
    while n_rows % tm:
        tm //= 2
    return tm


def _mod_spec(per_token, tm, tiles_per_seq, seg, ngrid):
    if per_token:
        if ngrid == 2:
            return pl.BlockSpec((tm, D_MODEL), lambda i, j: (i, seg))
        return pl.BlockSpec((tm, D_MODEL), lambda i: (i, seg))
    if ngrid == 2:
        return pl.BlockSpec((None, 1, D_MODEL), lambda i, j: (i // tiles_per_seq, 0, seg))
    return pl.BlockSpec((None, 1, D_MODEL), lambda i: (i // tiles_per_seq, 0, seg))


def _in_kernel(x_ref, sh_ref, sc_ref, nw_ref, w_ref, wab_ref, o_ref, g_ref, ab_ref, h_scr):
    j = pl.program_id(1)

    @pl.when(j == 0)
    def _():
        h = _rms(x_ref[...]) * nw_ref[...]
        h = h * (1.0 + sc_ref[...]) + sh_ref[...]
        hb = h.astype(BF16)
        h_scr[...] = hb
        ab_ref[...] = jnp.dot(hb, wab_ref[...], preferred_element_type=F32)

    acc = jnp.dot(h_scr[...], w_ref[j], preferred_element_type=F32)

    @pl.when(j < MIXER_TILES)
    def _():
        o_ref[...] = acc

    @pl.when(j >= MIXER_TILES)
    def _():
        g_ref[...] = jax.nn.sigmoid(acc).astype(BF16)


def _in_call(x2, mod, per_token, seq_len, norm_w, w_main, w_ab):
    T = x2.shape[0]
    tm = _row_tile(T if per_token else seq_len, 512 if per_token else 1024)
    tn = IN_TN
    tps = max(seq_len // tm, 1)
    last_gate = IN_TILES - MIXER_TILES - 1

    def gate_block(i, j):
        on_gate = j >= MIXER_TILES
        row = jnp.where(on_gate, i, jnp.maximum(i - 1, 0))
        col = jnp.where(on_gate, j - MIXER_TILES, jnp.where(i > 0, last_gate, 0))
        return row, col

    return pl.pallas_call(
        _in_kernel,
        grid=(T // tm, IN_TILES),
        in_specs=[
            pl.BlockSpec((tm, D_MODEL), lambda i, j: (i, 0)),
            _mod_spec(per_token, tm, tps, 0, 2),
            _mod_spec(per_token, tm, tps, 1, 2),
            pl.BlockSpec((1, D_MODEL), lambda i, j: (0, 0)),
            pl.BlockSpec((IN_TILES, D_MODEL, tn), lambda i, j: (0, 0, 0), pipeline_mode=pl.Buffered(1)),
            pl.BlockSpec((D_MODEL, AB_COLS), lambda i, j: (0, 0)),
        ],
        out_specs=[
            pl.BlockSpec((tm, tn), lambda i, j: (i, jnp.minimum(j, MIXER_TILES - 1))),
            pl.BlockSpec((tm, tn), gate_block),
            pl.BlockSpec((tm, AB_COLS), lambda i, j: (i, 0)),
        ],
        out_shape=[
            jax.ShapeDtypeStruct((T, MIXER_COLS), F32),
            jax.ShapeDtypeStruct((T, N_BRANCH * D_MODEL), BF16),
            jax.ShapeDtypeStruct((T, AB_COLS), F32),
        ],
        scratch_shapes=[pltpu.VMEM((tm, D_MODEL), BF16)],
        compiler_params=pltpu.CompilerParams(
            dimension_semantics=("arbitrary", "arbitrary"), vmem_limit_bytes=VMEM_LIMIT),
        name="in_proj",
    )(x2, mod, mod, norm_w.reshape(1, D_MODEL), w_main, w_ab)


def _mixer_kernel(*refs, C, Lv, nb, layer, passes, has_state, n_prev):
    proj_ref, ab_ref = refs[:2]
    pos = 2
    if has_state:
        conv0_ref, sg0_ref, sr0_ref, sh0_ref = refs[pos:pos + 4]
        pos += 4
    (cw_ref, hp_ref, gn_ref, hn_ref, lb_ref, cos_ref, sin_ref,
     dtab_ref, inner_ref, toend_ref, cdec_ref) = refs[pos:pos + 11]
    pos += 11
    prev_refs = refs[pos:pos + 3 * n_prev]
    ob_ref, sg_ref, sr_ref, sh_ref, xpad_ref, hg_ref = refs[pos + 3 * n_prev:]
    if n_prev:
        stacked = (sg_ref, sr_ref, sh_ref)
        sg_ref, sr_ref, sh_ref = (r.at[n_prev] for r in stacked)
    tail0 = SUBLANES - (CONV_W - 1)

    @pl.when(pl.program_id(1) == 0)
    def _():
        for p in range(n_prev):
            for k in range(3):
                stacked[k][p] = prev_refs[3 * p + k][...]
        if has_state:
            sg_ref[...] = sg0_ref[...]
            sr_ref[...] = sr0_ref[...]
            sh_ref[...] = sh0_ref[...]
            xpad_ref[:, tail0:SUBLANES, :] = conv0_ref[...]
        else:
            sg_ref[...] = jnp.zeros_like(sg_ref)
            sr_ref[...] = jnp.zeros_like(sr_ref)
            sh_ref[...] = jnp.zeros_like(sh_ref)
            xpad_ref[:, tail0:SUBLANES, :] = jnp.zeros((nb, CONV_W - 1, CONV_DIM), F32)

    padded = Lv < C
    row1 = lax.broadcasted_iota(jnp.int32, (C, 1), 0)
    valid = row1 < Lv
    ri = lax.broadcasted_iota(jnp.int32, (C, C), 0)
    ci = lax.broadcasted_iota(jnp.int32, (C, C), 1)
    incl = ri >= ci
    strict = ri > ci
    eye = ri == ci
    ltri = jnp.where(incl, 1.0, 0.0).astype(BF16)
    r128 = lax.broadcasted_iota(jnp.int32, (HEAD, HEAD), 0)
    c128 = lax.broadcasted_iota(jnp.int32, (HEAD, HEAD), 1)
    eye128 = r128 == c128
    n_levels = max((Lv - 1).bit_length(), 1)

    lbp = lb_ref[...]
    lbe = jnp.exp(lbp - jnp.max(lbp, axis=0, keepdims=True))
    lbw = lbe / jnp.sum(lbe, axis=0, keepdims=True)
    lb_row = lbw[0:1, :]
    for d in range(1, layer + 1):
        lb_row = lb_row + lbw[d:d + 1, :]
    lb_row = lb_row - lbw[0:1, :]

    neg_a = -jnp.exp(hp_ref[0:1, :])
    dt_b = hp_ref[1:2, :]
    gn = gn_ref[...]
    hn = hn_ref[...]
    cosf = cos_ref[...]
    sinf = sin_ref[...]

    SB = SUBLANES

    def gdn_task(n, h, qkv, G_all, beta_all):
        q = qkv[:, h * HEAD:(h + 1) * HEAD]
        k = qkv[:, MIX + h * HEAD:MIX + (h + 1) * HEAD]
        v = qkv[:, 2 * MIX + h * HEAD:2 * MIX + (h + 1) * HEAD]
        q = q * lax.rsqrt(jnp.sum(q * q, axis=-1, keepdims=True) + EPS) * (HEAD ** -0.5)
        k = k * lax.rsqrt(jnp.sum(k * k, axis=-1, keepdims=True) + EPS)
        kk = _dot(k, k, 'nt', passes)
        qk = _dot(q, k, 'nt', passes)
        yield
        Gc = G_all[:, h:h + 1]
        beta = beta_all[:, N_HEADS + h:N_HEADS + h + 1]
        Gr = jnp.sum(jnp.where(eye, Gc, 0.0), axis=0, keepdims=True)
        G_last = G_all[C - 1:C, h:h + 1]
        eG = jnp.exp(Gc)
        decay = jnp.where(incl, jnp.exp(jnp.where(incl, Gc - Gr, 0.0)), 0.0)
        P = -jnp.where(strict, beta * kk * decay, 0.0)
        sol = jnp.concatenate([beta * v, (beta * eG) * k], axis=-1)
        for lvl in range(n_levels):
            upd = _dot(P, sol, 'nn', passes)
            if lvl + 1 < n_levels:
                P = _dot(P, P, 'nn', passes)
            yield
            sol = sol + upd
        S = sg_ref[n, h]
        sk_s = _dot(sol[:, HEAD:], S, 'nn', passes)
        q_s = _dot(q, S, 'nn', passes)
        yield
        u = sol[:, :HEAD] - sk_s
        k_end = k * jnp.exp(G_last - Gc)
        intra = _dot(qk * decay, u, 'nn', passes)
        s_upd = _dot(k_end, u, 'tn', passes)
        yield
        o = eG * q_s + intra
        sg_ref[n, h] = jnp.exp(G_last) * S + s_upd
        z = proj_ref[n, :, GDN_OFF + 3 * MIX + h * HEAD:GDN_OFF + 3 * MIX + (h + 1) * HEAD]
        ob_ref[n, :, h * HEAD:(h + 1) * HEAD] = ((_rms(o) * gn) * _silu(z)).astype(ob_ref.dtype)

    def ret_task(n, h):
        base = RET_OFF + h * HEAD
        q = proj_ref[n, :, base:base + HEAD]
        k = proj_ref[n, :, base + MIX:base + MIX + HEAD]
        v = proj_ref[n, :, base + 2 * MIX:base + 2 * MIX + HEAD]
        q = q * cosf + pltpu.roll(q, HEAD // 2, 1) * sinf
        k = (k * cosf + pltpu.roll(k, HEAD // 2, 1) * sinf) * (HEAD ** -0.5)
        if padded:
            v = jnp.where(valid, v, 0.0)
        S = sr_ref[n, h]
        qk = _dot(q, k, 'nt', passes)
        q_s = _dot(q, S, 'nn', passes)
        s_upd = _dot(k * toend_ref[h], v, 'tn', passes)
        yield
        intra = _dot(qk * dtab_ref[h], v, 'nn', passes)
        sr_ref[n, h] = cdec_ref[h] * S + s_upd
        yield
        o = intra + inner_ref[h] * q_s
        gate = proj_ref[n, :, base + 3 * MIX:base + 3 * MIX + HEAD]
        mu = jnp.mean(o, axis=-1, keepdims=True)
        oc = o - mu
        var = jnp.mean(oc * oc, axis=-1, keepdims=True)
        ob_ref[n, :, MIX + h * HEAD:MIX + (h + 1) * HEAD] = (
            oc * lax.rsqrt(var + EPS) * _silu(gate)).astype(ob_ref.dtype)

    def hg_task(n, h):
        base = HG_OFF + h * HEAD
        qc = proj_ref[n, :, base:base + HEAD]
        zf = proj_ref[n, :, base + MIX:base + MIX + HEAD]
        vc = proj_ref[n, :, base + 2 * MIX:base + 2 * MIX + HEAD]
        lb_h = lb_row[:, h * HEAD:(h + 1) * HEAD]
        s_neg = jax.nn.sigmoid(-zf)
        f_gate = jax.nn.sigmoid(zf) + lb_h * s_neg
        log_f = jnp.log(jnp.maximum(f_gate, F_TINY))
        kc = (1.0 - lb_h) * s_neg
        if padded:
            log_f = jnp.where(valid, log_f, 0.0)
            kc = jnp.where(valid, kc, 0.0)
            vc = jnp.where(valid, vc, 0.0)
        A = _cumsum_rows(ltri, log_f)
        yield
        hgs = hg_ref.at[n, h]
        hgs[0] = A
        hgs[1] = kc
        S = sh_ref[n, h]
        a_last = hgs[0, C - 1:C, :]
        q_s = _dot(qc * jnp.exp(A), S, 'nn', passes)
        s_upd = _dot(kc * jnp.exp(a_last - A), vc, 'tn', passes)
        score_rows = []
        for blk in range(C // SB):
            r0 = blk * SB
            qI = qc[r0:r0 + SB]
            AI = A[r0:r0 + SB]
            rowg = r0 + lax.broadcasted_iota(jnp.int32, (SB, C), 0)
            colg = lax.broadcasted_iota(jnp.int32, (SB, C), 1)
            cross = None
            if r0 > 0:
                a_b = hgs[0, r0 - 1:r0, :]
                kt = jnp.concatenate([kc[:r0] * jnp.exp(a_b - A[:r0]), jnp.zeros((C - r0, HEAD), F32)], axis=0)
                qt = qI * jnp.exp(AI - a_b)
                cross = _dot(qt, kt, 'nt', passes)
            sc_blk = jnp.zeros((SB, C), F32)
            for s in range(SB):
                a_s = hgs[0, r0 + s:r0 + s + 1, :]
                k_s = hgs[1, r0 + s:r0 + s + 1, :]
                w = jnp.exp(AI - a_s)
                val = jnp.sum(qI * k_s * w, axis=-1, keepdims=True)
                sc_blk = jnp.where(colg == r0 + s, val, sc_blk)
            yield
            if cross is not None:
                sc_blk = sc_blk + cross
            score_rows.append(jnp.where(rowg >= colg, sc_blk, 0.0))
        scores = score_rows[0] if len(score_rows) == 1 else jnp.concatenate(score_rows, axis=0)
        intra = _dot(scores, vc, 'nn', passes)
        dec_col = jnp.sum(jnp.where(eye128, jnp.exp(a_last), 0.0), axis=1, keepdims=True)
        sh_ref[n, h] = dec_col * S + s_upd
        yield
        o = intra + q_s
        gate = proj_ref[n, :, base + 3 * MIX:base + 3 * MIX + HEAD]
        ob_ref[n, :, 2 * MIX + h * HEAD:2 * MIX + (h + 1) * HEAD] = (
            (_rms(o) * hn) * jax.nn.sigmoid(gate)).astype(ob_ref.dtype)

    gdn_tasks, other_tasks = [], []
    for n in range(nb):
        xpad_ref[n, SUBLANES:SUBLANES + C, :] = proj_ref[n, :, GDN_OFF:GDN_OFF + CONV_DIM]
        conv = xpad_ref[n, SUBLANES:SUBLANES + C, :] * cw_ref[CONV_W - 1:CONV_W, :]
        for j in range(CONV_W - 1):
            off = SUBLANES - (CONV_W - 1) + j
            conv = conv + xpad_ref[n, off:off + C, :] * cw_ref[j:j + 1, :]
        xpad_ref[n, 0:SUBLANES, :] = xpad_ref[n, C:C + SUBLANES, :]
        qkv = _silu(conv)
        ab = ab_ref[n]
        g_all = neg_a * jax.nn.softplus(ab + dt_b)
        beta_all = jax.nn.sigmoid(ab)
        if padded:
            g_all = jnp.where(valid, g_all, 0.0)
            beta_all = jnp.where(valid, beta_all, 0.0)
        G_all = _cumsum_rows(ltri, g_all)
        for h in range(N_HEADS):
            gdn_tasks.append(gdn_task(n, h, qkv, G_all, beta_all))
            other_tasks.append(hg_task(n, h))
            other_tasks.append(ret_task(n, h))
    tasks = gdn_tasks + other_tasks
    while tasks:
        alive = []
        for t in tasks:
            try:
                next(t)
                alive.append(t)
            except StopIteration:
                pass
        tasks = alive


def _mixer_call(proj3, ab3, states, prev_states, conv_w, head_params, gdn_norm, hg_norm, hgrn_lb,
                cosf, sinf, dtab, inner, toend, cdec, *, C, Lv, nb, layer, passes):
    B, Lp, _ = proj3.shape
    n_chunks = Lp // C
    depth = hgrn_lb.shape[0]
    n_prev = len(prev_states)
    in_state_spec = pl.BlockSpec((None, nb, N_HEADS, HEAD, HEAD), lambda b, i: (layer, b, 0, 0, 0))
    layer_state_spec = pl.BlockSpec((nb, N_HEADS, HEAD, HEAD), lambda b, i: (b, 0, 0, 0))
    if n_prev:
        out_state_spec = pl.BlockSpec((n_prev + 1, nb, N_HEADS, HEAD, HEAD), lambda b, i: (0, b, 0, 0, 0))
        state_shape = jax.ShapeDtypeStruct((n_prev + 1, B, N_HEADS, HEAD, HEAD), F32)
    else:
        out_state_spec = layer_state_spec
        state_shape = jax.ShapeDtypeStruct((B, N_HEADS, HEAD, HEAD), F32)
    ob_dtype = BF16 if C % (2 * SUBLANES) == 0 else F32

    def whole(shape):
        return pl.BlockSpec(shape, lambda b, i: (0,) * len(shape))

    has_state = states is not None
    operands = [proj3, ab3]
    in_specs = [
        pl.BlockSpec((nb, C, MIXER_COLS), lambda b, i: (b, i, 0)),
        pl.BlockSpec((nb, C, AB_COLS), lambda b, i: (b, i, 0)),
    ]
    if has_state:
        operands += [states[1], states[0], states[2], states[3]]
        in_specs += [pl.BlockSpec((None, nb, CONV_W - 1, CONV_DIM), lambda b, i: (layer, b, 0, 0)),
                     in_state_spec, in_state_spec, in_state_spec]
    operands += [conv_w, head_params, gdn_norm.reshape(1, HEAD), hg_norm.reshape(1, HEAD), hgrn_lb,
                 cosf, sinf, dtab, inner, toend, cdec]

    prev_flat = [a for group in prev_states for a in group]
    kern = functools.partial(_mixer_kernel, C=C, Lv=Lv, nb=nb, layer=layer, passes=passes,
                             has_state=has_state, n_prev=n_prev)
    return pl.pallas_call(
        kern,
        grid=(B // nb, n_chunks),
        in_specs=in_specs + [
            whole((CONV_W, CONV_DIM)),
            whole((2, AB_COLS)),
            whole((1, HEAD)),
            whole((1, HEAD)),
            whole((depth, MIX)),
            pl.BlockSpec((C, HEAD), lambda b, i: (i, 0)),
            pl.BlockSpec((C, HEAD), lambda b, i: (i, 0)),
            whole((N_HEADS, C, C)),
            whole((N_HEADS, C, HEAD)),
            whole((N_HEADS, C, HEAD)),
            whole((N_HEADS, 1, HEAD)),
        ] + [layer_state_spec] * len(prev_flat),
        out_specs=[
            pl.BlockSpec((nb, C, N_BRANCH * MIX), lambda b, i: (b, i, 0)),
            out_state_spec, out_state_spec, out_state_spec,
        ],
        out_shape=[
            jax.ShapeDtypeStruct((B, Lp, N_BRANCH * MIX), ob_dtype),
            state_shape, state_shape, state_shape,
        ],
        scratch_shapes=[
            pltpu.VMEM((nb, C + SUBLANES, CONV_DIM), F32),
            pltpu.VMEM((nb, N_HEADS, 2, C, HEAD), F32),
        ],
        compiler_params=pltpu.CompilerParams(
            dimension_semantics=("arbitrary", "arbitrary"), vmem_limit_bytes=VMEM_LIMIT),
        name="mixer",
    )(*operands, *prev_flat)


def _post_kernel(ob_ref, mg_ref, x_ref, gt_ref, wb_ref, wo_ref, o_ref):
    merged = None
    for n in range(N_BRANCH):
        br = _dot(ob_ref[:, n * MIX:(n + 1) * MIX], wb_ref[n])
        term = mg_ref[:, n * D_MODEL:(n + 1) * D_MODEL].astype(F32) * br
        merged = term if merged is None else merged + term
    o_ref[...] = x_ref[...] + gt_ref[...] * _dot(merged, wo_ref[...])


def _post_call(ob2, gates, x2, mod, per_token, seq_len, w_branch, w_out):
    T = x2.shape[0]
    tm = _row_tile(T if per_token else seq_len, 512)
    tps = max(seq_len // tm, 1)
    return pl.pallas_call(
        _post_kernel,
        grid=(T // tm,),
        in_specs=[
            pl.BlockSpec((tm, N_BRANCH * MIX), lambda i: (i, 0)),
            pl.BlockSpec((tm, N_BRANCH * D_MODEL), lambda i: (i, 0)),
            pl.BlockSpec((tm, D_MODEL), lambda i: (i, 0)),
            _mod_spec(per_token, tm, tps, 2, 1),
            pl.BlockSpec((N_BRANCH, MIX, D_MODEL), lambda i: (0, 0, 0)),
            pl.BlockSpec((D_MODEL, D_MODEL), lambda i: (0, 0)),
        ],
        out_specs=pl.BlockSpec((tm, D_MODEL), lambda i: (i, 0)),
        out_shape=jax.ShapeDtypeStruct((T, D_MODEL), F32),
        compiler_params=pltpu.CompilerParams(
            dimension_semantics=("arbitrary",), vmem_limit_bytes=VMEM_LIMIT),
        name="post",
    )(ob2, gates, x2, mod, w_branch, w_out)


FF_SPLITS = (0, 1024, 2048, D_FF)


def _ffn_kernel(x_ref, sh_ref, sc_ref, gt_ref, nw_ref, fw_ref, wg_ref, wu_ref, wd_ref, o_ref, *, final):
    x = x_ref[...]
    h = (_rms(x) * nw_ref[...] * (1.0 + sc_ref[...]) + sh_ref[...]).astype(BF16)
    y = None
    for lo, hi in zip(FF_SPLITS[:-1], FF_SPLITS[1:]):
        g = jnp.dot(h, wg_ref[:, lo:hi], preferred_element_type=F32)
        u = jnp.dot(h, wu_ref[:, lo:hi], preferred_element_type=F32)
        part = _dot(_silu(g) * u, wd_ref[lo:hi, :])
        y = part if y is None else y + part
    xn = x + gt_ref[...] * y
    if final:
        xn = _rms(xn) * fw_ref[...]
    o_ref[...] = xn


def _ffn_call(x2, mod, per_token, seq_len, norm_w, final_w, w_gate_up, w_down, final):
    T = x2.shape[0]
    tm = _row_tile(T if per_token else seq_len, 512)
    tps = max(seq_len // tm, 1)
    resident = pl.Buffered(1)
    return pl.pallas_call(
        functools.partial(_ffn_kernel, final=final),
        grid=(T // tm,),
        in_specs=[
            pl.BlockSpec((tm, D_MODEL), lambda i: (i, 0)),
            _mod_spec(per_token, tm, tps, 3, 1),
            _mod_spec(per_token, tm, tps, 4, 1),
            _mod_spec(per_token, tm, tps, 5, 1),
            pl.BlockSpec((1, D_MODEL), lambda i: (0, 0)),
            pl.BlockSpec((1, D_MODEL), lambda i: (0, 0)),
            pl.BlockSpec((D_MODEL, D_FF), lambda i: (0, 0), pipeline_mode=resident),
            pl.BlockSpec((D_MODEL, D_FF), lambda i: (0, 1), pipeline_mode=resident),
            pl.BlockSpec((D_FF, D_MODEL), lambda i: (0, 0), pipeline_mode=resident),
        ],
        out_specs=pl.BlockSpec((tm, D_MODEL), lambda i: (i, 0)),
        out_shape=jax.ShapeDtypeStruct((T, D_MODEL), F32),
        compiler_params=pltpu.CompilerParams(
            dimension_semantics=("arbitrary",), vmem_limit_bytes=VMEM_LIMIT),
        name="ffn",
    )(x2, mod, mod, mod, norm_w.reshape(1, D_MODEL), final_w.reshape(1, D_MODEL),
      w_gate_up, w_gate_up, w_down)


def _rope_tables(pos0, length, padded_len):
    half = HEAD // 2
    pos = pos0 + jnp.arange(padded_len, dtype=jnp.int32)
    inv = ROPE_BASE ** (-jnp.arange(half, dtype=F32) / half)
    ang = pos.astype(F32)[:, None] * inv[None, :]
    cos, sin = jnp.cos(ang), jnp.sin(ang)
    return jnp.concatenate([cos, cos], axis=-1), jnp.concatenate([-sin, sin], axis=-1)


def _retention_tables(c_real, c_pad):
    log_gamma = jnp.log1p(-jnp.exp2(-5.0 - jnp.arange(N_HEADS, dtype=F32)))
    idx = jnp.arange(c_real, dtype=F32)
    rel = idx[:, None] - idx[None, :]
    mask = rel >= 0
    lg = log_gamma[:, None, None]
    dtab = jnp.where(mask, jnp.exp(jnp.where(mask, lg * rel, 0.0)), 0.0)
    inner = jnp.exp(log_gamma[:, None] * (idx + 1.0))
    to_end = jnp.exp(log_gamma[:, None] * (c_real - 1.0 - idx))
    cdec = jnp.exp(log_gamma * c_real)
    p = c_pad - c_real
    dtab = jnp.pad(dtab, ((0, 0), (0, p), (0, p)))
    inner = jnp.broadcast_to(jnp.pad(inner, ((0, 0), (0, p)))[:, :, None], (N_HEADS, c_pad, HEAD))
    to_end = jnp.broadcast_to(jnp.pad(to_end, ((0, 0), (0, p)))[:, :, None], (N_HEADS, c_pad, HEAD))
    cdec = jnp.broadcast_to(cdec[:, None, None], (N_HEADS, 1, HEAD))
    return dtab, inner, to_end, cdec


def _trunk(x, mod_all, states, pos0, wts, nb, nb_last, passes):
    B, L, _ = x.shape
    depth = mod_all.shape[0]
    c_real = CHUNK if L % CHUNK == 0 else L
    assert L % c_real == 0 and c_real <= CHUNK
    C = -(-c_real // SUBLANES) * SUBLANES
    Lp = L if C == c_real else C
    assert Lp == L or L == c_real
    per_token = Lp < 256
    if Lp != L:
        x = jnp.pad(x, ((0, 0), (0, Lp - L), (0, 0)))
    T = B * Lp
    x2 = x.reshape(T, D_MODEL)
    cosf, sinf = _rope_tables(pos0, L, Lp)
    dtab, inner, toend, cdec = _retention_tables(c_real, C)

    tails, layer_states = [], []
    for l in range(depth):
        last = l == depth - 1
        if per_token:
            mod = jnp.repeat(mod_all[l], Lp, axis=0)
        else:
            mod = mod_all[l].reshape(B, 1, 6 * D_MODEL)
        proj, gates, ab = _in_call(x2, mod, per_token, Lp, wts['norm_mix'][l], wts['w_main'][l], wts['w_ab'][l])
        proj3 = proj.reshape(B, Lp, MIXER_COLS)
        ob, sg, sr, sh = _mixer_call(
            proj3, ab.reshape(B, Lp, AB_COLS), states, layer_states if last else [],
            wts['conv_w'][l], wts['head_params'][l], wts['gdn_norm'][l], wts['hgrn_norm'][l], wts['hgrn_lb'],
            cosf, sinf, dtab, inner, toend, cdec, C=C, Lv=c_real, nb=nb_last if last else nb, layer=l,
            passes=passes)
        tail = proj3[:, L - min(L, CONV_W - 1):L, GDN_OFF:GDN_OFF + CONV_DIM]
        if L < CONV_W - 1:
            tail = jnp.concatenate([states[1][l][:, L:], tail], axis=1)
        tails.append(tail)
        layer_states.append((sg, sr, sh))
        x1 = _post_call(ob.reshape(T, N_BRANCH * MIX), gates, x2, mod, per_token, Lp,
                        wts['w_branch'][l], wts['w_out'][l])
        x2 = _ffn_call(x1, mod, per_token, Lp, wts['norm_ffn'][l], wts['final_norm'],
                       wts['w_gate_up'][l], wts['w_down'][l], final=(l == depth - 1))
    y = x2.reshape(B, Lp, D_MODEL)[:, :L]
    sg, sr, sh = layer_states[-1]
    if depth == 1:
        sg, sr, sh = sg[None], sr[None], sh[None]
    return y, (sg, jnp.stack(tails), sr, sh)


def _prep_weights(w_in, conv_w, gdn_a_log, gdn_dt_bias, gdn_norm, hgrn_lb, hgrn_norm, w_branch, w_out,
                  norm_mix, norm_ffn, w_gate_up, w_down, final_norm):
    ab0 = 4 * MIX
    skip = 2 * N_HEADS
    tiles = []
    for t in range(IN_TILES):
        lo, hi = t * IN_TN, (t + 1) * IN_TN
        pieces = []
        if lo < ab0:
            pieces.append(w_in[:, :, lo:min(hi, ab0)])
        if hi > ab0:
            pieces.append(w_in[:, :, max(lo, ab0) + skip:hi + skip])
        tiles.append(pieces[0] if len(pieces) == 1 else jnp.concatenate(pieces, axis=-1))
    w_main = jnp.stack(tiles, axis=1).astype(BF16)
    w_ab = jnp.pad(w_in[:, :, ab0:ab0 + 2 * N_HEADS],
                   ((0, 0), (0, 0), (0, AB_COLS - 2 * N_HEADS))).astype(BF16)
    head_params = jnp.pad(jnp.stack([gdn_a_log, gdn_dt_bias], axis=1),
                          ((0, 0), (0, 0), (0, AB_COLS - N_HEADS))).astype(F32)
    return dict(
        w_main=w_main, w_ab=w_ab, conv_w=conv_w, head_params=head_params, gdn_norm=gdn_norm,
        hgrn_norm=hgrn_norm, hgrn_lb=hgrn_lb, w_branch=w_branch.astype(BF16), w_out=w_out.astype(BF16),
        norm_mix=norm_mix, norm_ffn=norm_ffn, w_gate_up=w_gate_up.astype(BF16),
        w_down=w_down.astype(BF16), final_norm=final_norm)


def kernel(x_prompt, x_sample, state_gdn, state_gdn_conv, state_ret, state_hgrn, c_prompt, c_sample,
           w_in, conv_w, gdn_a_log, gdn_dt_bias, gdn_norm, hgrn_lb, hgrn_norm, w_branch, w_out,
           w_ada, b_ada, norm_mix, norm_ffn, w_gate_up, w_down, final_norm):
    Bp = x_prompt.shape[0]
    wts = _prep_weights(w_in, conv_w, gdn_a_log, gdn_dt_bias, gdn_norm, hgrn_lb, hgrn_norm, w_branch, w_out,
                        norm_mix, norm_ffn, w_gate_up, w_down, final_norm)
    c_all = jnp.concatenate([c_prompt, c_sample], axis=0)
    mod_all = _ada_call(c_all, w_ada, b_ada)
    y_p, st_p = _trunk(x_prompt, mod_all[:, :Bp], None, 0, wts, nb=2, nb_last=2, passes=1)
    y_s, st_s = _trunk(x_sample, mod_all[:, Bp:], (state_gdn, state_gdn_conv, state_ret, state_hgrn),
                       PAST_LEN, wts, nb=8, nb_last=4, passes=1)
    return (y_p, y_s) + st_p + st_s
```

```python
import functools

import jax
import jax.numpy as jnp
from jax import lax
from jax.experimental import pallas as pl
from jax.experimental.pallas import tpu as pltpu

F32 = jnp.float32
BF16 = jnp.bfloat16

D_MODEL = 1024
N_HEADS = 4
HEAD = 128
MIX = N_HEADS * HEAD
CONV_W = 4
CONV_DIM = 3 * MIX
CHUNK = 64
N_BRANCH = 3
D_FF = 2816
ROPE_BASE = 10000.0
EPS = 1e-6
F_TINY = 1e-30
PAST_LEN = 16384

GDN_OFF = 0
RET_OFF = 4 * MIX
HG_OFF = 8 * MIX
MIXER_COLS = 12 * MIX
MERGE_OFF = MIXER_COLS
MAIN_COLS = MIXER_COLS + N_BRANCH * D_MODEL
AB_COLS = 128
IN_TN = 1536
IN_TILES = MAIN_COLS // IN_TN
MIXER_TILES = MIXER_COLS // IN_TN

SUBLANES = 8
VMEM_LIMIT = 56 * 1024 * 1024

_DN = {
    'nn': (((1,), (0,)), ((), ())),
    'nt': (((1,), (1,)), ((), ())),
    'tn': (((0,), (0,)), ((), ())),
}


def _split2(a):
    hi = a.astype(BF16)
    lo = (a - hi.astype(F32)).astype(BF16)
    return hi, lo


def _dot(a, b, dims='nn', passes=1):
    dn = _DN[dims]
    if passes == 1:
        return lax.dot_general(a.astype(BF16), b.astype(BF16), dn, preferred_element_type=F32)
    a_hi, a_lo = _split2(a)
    b_hi, b_lo = _split2(b)
    out = lax.dot_general(a_hi, b_lo, dn, preferred_element_type=F32)
    out = out + lax.dot_general(a_lo, b_hi, dn, preferred_element_type=F32)
    return out + lax.dot_general(a_hi, b_hi, dn, preferred_element_type=F32)


def _cumsum_rows(ltri_bf16, x):
    x1 = x.astype(BF16)
    r1 = x - x1.astype(F32)
    x2 = r1.astype(BF16)
    x3 = (r1 - x2.astype(F32)).astype(BF16)
    dn = _DN['nn']
    out = lax.dot_general(ltri_bf16, x3, dn, preferred_element_type=F32)
    out = out + lax.dot_general(ltri_bf16, x2, dn, preferred_element_type=F32)
    return out + lax.dot_general(ltri_bf16, x1, dn, preferred_element_type=F32)


def _silu(x):
    return x * jax.nn.sigmoid(x)


def _rms(x):
    return x * lax.rsqrt(jnp.mean(x * x, axis=-1, keepdims=True) + EPS)


def _ada_kernel(c_ref, w_ref, b_ref, o_ref):
    cs = _silu(c_ref[...])
    o_ref[...] = _dot(cs, w_ref[...]) + b_ref[...]


def _ada_call(c_all, w_ada, b_ada):
    depth = w_ada.shape[0]
    rows = c_all.shape[0]
    n_out = w_ada.shape[2]
    tn = 1536
    return pl.pallas_call(
        _ada_kernel,
        grid=(depth, n_out // tn),
        in_specs=[
            pl.BlockSpec((rows, D_MODEL), lambda l, j: (0, 0)),
            pl.BlockSpec((None, D_MODEL, tn), lambda l, j: (l, 0, j)),
            pl.BlockSpec((None, 1, tn), lambda l, j: (l, 0, j)),
        ],
        out_specs=pl.BlockSpec((None, rows, tn), lambda l, j: (l, 0, j)),
        out_shape=jax.ShapeDtypeStruct((depth, rows, n_out), F32),
        compiler_params=pltpu.CompilerParams(
            dimension_semantics=("arbitrary", "arbitrary"), vmem_limit_bytes=VMEM_LIMIT),
        name="ada",
    )(c_all, w_ada, b_ada.reshape(depth, 1, n_out))


def _row_tile(n_rows, preferred):
    tm = preferred
    while n_rows % tm:
        tm //= 2
    return tm


def _mod_spec(per_token, tm, tiles_per_seq, seg, ngrid):
    if per_token:
        if ngrid == 2:
            return pl.BlockSpec((tm, D_MODEL), lambda i, j: (i, seg))
        return pl.BlockSpec((tm, D_MODEL), lambda i: (i, seg))
    if ngrid == 2:
        return pl.BlockSpec((None, 1, D_MODEL), lambda i, j: (i // tiles_per_seq, 0, seg))
    return pl.BlockSpec((None, 1, D_MODEL), lambda i: (i // tiles_per_seq, 0, seg))


def _in_kernel(*refs, fuse_conv, tiles_per_seq):
    if fuse_conv:
        (x_ref, sh_ref, sc_ref, nw_ref, w_ref, wab_ref, cw_ref,
         o_ref, g_ref, ab_ref, tail_ref, h_scr, xp_scr, carry_scr) = refs
    else:
        x_ref, sh_ref, sc_ref, nw_ref, w_ref, wab_ref, o_ref, g_ref, ab_ref, h_scr = refs
    i = pl.program_id(0)
    j = pl.program_id(1)
    tm = x_ref.shape[0]

    @pl.when(j == 0)
    def _():
        h = _rms(x_ref[...]) * nw_ref[...]
        h = h * (1.0 + sc_ref[...]) + sh_ref[...]
        hb = h.astype(BF16)
        h_scr[...] = hb
        ab_ref[...] = jnp.dot(hb, wab_ref[...], preferred_element_type=F32)
        if not fuse_conv:
            o_ref[...] = jnp.dot(hb, w_ref[0], preferred_element_type=F32)
            return

        @pl.when(i % tiles_per_seq == 0)
        def _():
            carry_scr[...] = jnp.zeros_like(carry_scr)

        for c in range(3):
            lo = c * MIX
            raw = jnp.dot(hb, w_ref[0, :, lo:lo + MIX], preferred_element_type=F32)
            xp_scr[0:SUBLANES, :] = carry_scr[:, lo:lo + MIX]
            xp_scr[SUBLANES:SUBLANES + tm, :] = raw
            conv = xp_scr[SUBLANES:SUBLANES + tm, :] * cw_ref[CONV_W - 1:CONV_W, lo:lo + MIX]
            for t in range(CONV_W - 1):
                off = SUBLANES - (CONV_W - 1) + t
                conv = conv + xp_scr[off:off + tm, :] * cw_ref[t:t + 1, lo:lo + MIX]
            carry_scr[:, lo:lo + MIX] = xp_scr[tm:tm + SUBLANES, :]
            act = _silu(conv)
            if c == 2:
                o_ref[:, lo:lo + MIX] = act
                continue
            for hh in range(N_HEADS):
                a = act[:, hh * HEAD:(hh + 1) * HEAD]
                a = a * lax.rsqrt(jnp.sum(a * a, axis=-1, keepdims=True) + EPS)
                if c == 0:
                    a = a * (HEAD ** -0.5)
                o_ref[:, lo + hh * HEAD:lo + (hh + 1) * HEAD] = a
        tail_ref[...] = carry_scr[...]

    @pl.when((j > 0) & (j < MIXER_TILES))
    def _():
        o_ref[...] = jnp.dot(h_scr[...], w_ref[j], preferred_element_type=F32)

    @pl.when(j >= MIXER_TILES)
    def _():
        acc = jnp.dot(h_scr[...], w_ref[j], preferred_element_type=F32)
        g_ref[...] = jax.nn.sigmoid(acc).astype(BF16)


def _in_call(x2, mod, per_token, seq_len, norm_w, w_main, w_ab, conv_w, fuse_conv):
    T = x2.shape[0]
    tm = _row_tile(T if per_token else seq_len, 512 if per_token else 1024)
    tn = IN_TN
    tps = max(seq_len // tm, 1)
    last_gate = IN_TILES - MIXER_TILES - 1
    assert not fuse_conv or (not per_token and IN_TN == CONV_DIM and GDN_OFF == 0)

    def gate_block(i, j):
        on_gate = j >= MIXER_TILES
        row = jnp.where(on_gate, i, jnp.maximum(i - 1, 0))
        col = jnp.where(on_gate, j - MIXER_TILES, jnp.where(i > 0, last_gate, 0))
        return row, col

    in_specs = [
        pl.BlockSpec((tm, D_MODEL), lambda i, j: (i, 0)),
        _mod_spec(per_token, tm, tps, 0, 2),
        _mod_spec(per_token, tm, tps, 1, 2),
        pl.BlockSpec((1, D_MODEL), lambda i, j: (0, 0)),
        pl.BlockSpec((IN_TILES, D_MODEL, tn), lambda i, j: (0, 0, 0), pipeline_mode=pl.Buffered(1)),
        pl.BlockSpec((D_MODEL, AB_COLS), lambda i, j: (0, 0)),
    ]
    out_specs = [
        pl.BlockSpec((tm, tn), lambda i, j: (i, jnp.minimum(j, MIXER_TILES - 1))),
        pl.BlockSpec((tm, tn), gate_block),
        pl.BlockSpec((tm, AB_COLS), lambda i, j: (i, 0)),
    ]
    out_shape = [
        jax.ShapeDtypeStruct((T, MIXER_COLS), F32),
        jax.ShapeDtypeStruct((T, N_BRANCH * D_MODEL), BF16),
        jax.ShapeDtypeStruct((T, AB_COLS), F32),
    ]
    scratch = [pltpu.VMEM((tm, D_MODEL), BF16)]
    operands = [x2, mod, mod, norm_w.reshape(1, D_MODEL), w_main, w_ab]
    if fuse_conv:
        in_specs.append(pl.BlockSpec((CONV_W, CONV_DIM), lambda i, j: (0, 0)))
        operands.append(conv_w)
        out_specs.append(pl.BlockSpec((SUBLANES, CONV_DIM), lambda i, j: (i, 0)))
        out_shape.append(jax.ShapeDtypeStruct((T // tm * SUBLANES, CONV_DIM), F32))
        scratch += [pltpu.VMEM((tm + SUBLANES, MIX), F32), pltpu.VMEM((SUBLANES, CONV_DIM), F32)]
    return pl.pallas_call(
        functools.partial(_in_kernel, fuse_conv=fuse_conv, tiles_per_seq=tps),
        grid=(T // tm, IN_TILES),
        in_specs=in_specs,
        out_specs=out_specs,
        out_shape=out_shape,
        scratch_shapes=scratch,
        compiler_params=pltpu.CompilerParams(
            dimension_semantics=("arbitrary", "arbitrary"), vmem_limit_bytes=VMEM_LIMIT),
        name="in_proj",
    )(*operands)


def _mixer_kernel(*refs, C, Lv, nb, layer, passes, has_state, n_prev, pre_conv):
    proj_ref, ab_ref = refs[:2]
    pos = 2
    if has_state:
        conv0_ref, sg0_ref, sr0_ref, sh0_ref = refs[pos:pos + 4]
        pos += 4
    (cw_ref, hp_ref, gn_ref, hn_ref, lb_ref, cos_ref, sin_ref,
     dtab_ref, inner_ref, toend_ref, cdec_ref) = refs[pos:pos + 11]
    pos += 11
    prev_refs = refs[pos:pos + 3 * n_prev]
    ob_ref, sg_ref, sr_ref, sh_ref, hg_ref = refs[pos + 3 * n_prev:pos + 3 * n_prev + 5]
    xpad_ref = None if pre_conv else refs[-1]
    if n_prev:
        stacked = (sg_ref, sr_ref, sh_ref)
        sg_ref, sr_ref, sh_ref = (r.at[n_prev] for r in stacked)
    tail0 = SUBLANES - (CONV_W - 1)

    @pl.when(pl.program_id(1) == 0)
    def _():
        for p in range(n_prev):
            for k in range(3):
                stacked[k][p] = prev_refs[3 * p + k][...]
        if has_state:
            sg_ref[...] = sg0_ref[...]
            sr_ref[...] = sr0_ref[...]
            sh_ref[...] = sh0_ref[...]
            xpad_ref[:, tail0:SUBLANES, :] = conv0_ref[...]
        else:
            sg_ref[...] = jnp.zeros_like(sg_ref)
            sr_ref[...] = jnp.zeros_like(sr_ref)
            sh_ref[...] = jnp.zeros_like(sh_ref)
            if not pre_conv:
                xpad_ref[:, tail0:SUBLANES, :] = jnp.zeros((nb, CONV_W - 1, CONV_DIM), F32)

    padded = Lv < C
    row1 = lax.broadcasted_iota(jnp.int32, (C, 1), 0)
    valid = row1 < Lv
    ri = lax.broadcasted_iota(jnp.int32, (C, C), 0)
    ci = lax.broadcasted_iota(jnp.int32, (C, C), 1)
    incl = ri >= ci
    strict = ri > ci
    eye = ri == ci
    ltri = jnp.where(incl, 1.0, 0.0).astype(BF16)
    r128 = lax.broadcasted_iota(jnp.int32, (HEAD, HEAD), 0)
    c128 = lax.broadcasted_iota(jnp.int32, (HEAD, HEAD), 1)
    eye128 = r128 == c128
    n_levels = max((Lv - 1).bit_length(), 1)

    lbp = lb_ref[...]
    lbe = jnp.exp(lbp - jnp.max(lbp, axis=0, keepdims=True))
    lbw = lbe / jnp.sum(lbe, axis=0, keepdims=True)
    lb_row = lbw[0:1, :]
    for d in range(1, layer + 1):
        lb_row = lb_row + lbw[d:d + 1, :]
    lb_row = lb_row - lbw[0:1, :]

    neg_a = -jnp.exp(hp_ref[0:1, :])
    dt_b = hp_ref[1:2, :]
    gn = gn_ref[...]
    hn = hn_ref[...]
    cosf = cos_ref[...]
    sinf = sin_ref[...]

    SB = SUBLANES

    def gdn_task(n, h, qkv, G_all, beta_all):
        if pre_conv:
            q = proj_ref[n, :, GDN_OFF + h * HEAD:GDN_OFF + (h + 1) * HEAD]
            k = proj_ref[n, :, GDN_OFF + MIX + h * HEAD:GDN_OFF + MIX + (h + 1) * HEAD]
            v = proj_ref[n, :, GDN_OFF + 2 * MIX + h * HEAD:GDN_OFF + 2 * MIX + (h + 1) * HEAD]
        else:
            q = qkv[:, h * HEAD:(h + 1) * HEAD]
            k = qkv[:, MIX + h * HEAD:MIX + (h + 1) * HEAD]
            v = qkv[:, 2 * MIX + h * HEAD:2 * MIX + (h + 1) * HEAD]
            q = q * lax.rsqrt(jnp.sum(q * q, axis=-1, keepdims=True) + EPS) * (HEAD ** -0.5)
            k = k * lax.rsqrt(jnp.sum(k * k, axis=-1, keepdims=True) + EPS)
        kk = _dot(k, k, 'nt', passes)
        qk = _dot(q, k, 'nt', passes)
        yield
        Gc = G_all[:, h:h + 1]
        beta = beta_all[:, N_HEADS + h:N_HEADS + h + 1]
        Gr = jnp.sum(jnp.where(eye, Gc, 0.0), axis=0, keepdims=True)
        G_last = G_all[C - 1:C, h:h + 1]
        eG = jnp.exp(Gc)
        decay = jnp.where(incl, jnp.exp(jnp.where(incl, Gc - Gr, 0.0)), 0.0)
        P = -jnp.where(strict, beta * kk * decay, 0.0)
        sol = jnp.concatenate([beta * v, (beta * eG) * k], axis=-1)
        for lvl in range(n_levels):
            upd = _dot(P, sol, 'nn', passes)
            if lvl + 1 < n_levels:
                P = _dot(P, P, 'nn', passes)
            yield
            sol = sol + upd
        S = sg_ref[n, h]
        sk_s = _dot(sol[:, HEAD:], S, 'nn', passes)
        q_s = _dot(q, S, 'nn', passes)
        yield
        u = sol[:, :HEAD] - sk_s
        k_end = k * jnp.exp(G_last - Gc)
        intra = _dot(qk * decay, u, 'nn', passes)
        s_upd = _dot(k_end, u, 'tn', passes)
        yield
        o = eG * q_s + intra
        sg_ref[n, h] = jnp.exp(G_last) * S + s_upd
        z = proj_ref[n, :, GDN_OFF + 3 * MIX + h * HEAD:GDN_OFF + 3 * MIX + (h + 1) * HEAD]
        ob_ref[n, :, h * HEAD:(h + 1) * HEAD] = ((_rms(o) * gn) * _silu(z)).astype(ob_ref.dtype)

    def ret_task(n, h):
        base = RET_OFF + h * HEAD
        q = proj_ref[n, :, base:base + HEAD]
        k = proj_ref[n, :, base + MIX:base + MIX + HEAD]
        v = proj_ref[n, :, base + 2 * MIX:base + 2 * MIX + HEAD]
        q = q * cosf + pltpu.roll(q, HEAD // 2, 1) * sinf
        k = (k * cosf + pltpu.roll(k, HEAD // 2, 1) * sinf) * (HEAD ** -0.5)
        if padded:
            v = jnp.where(valid, v, 0.0)
        S = sr_ref[n, h]
        qk = _dot(q, k, 'nt', passes)
        q_s = _dot(q, S, 'nn', passes)
        s_upd = _dot(k * toend_ref[h], v, 'tn', passes)
        yield
        intra = _dot(qk * dtab_ref[h], v, 'nn', passes)
        sr_ref[n, h] = cdec_ref[h] * S + s_upd
        yield
        o = intra + inner_ref[h] * q_s
        gate = proj_ref[n, :, base + 3 * MIX:base + 3 * MIX + HEAD]
        mu = jnp.mean(o, axis=-1, keepdims=True)
        oc = o - mu
        var = jnp.mean(oc * oc, axis=-1, keepdims=True)
        ob_ref[n, :, MIX + h * HEAD:MIX + (h + 1) * HEAD] = (
            oc * lax.rsqrt(var + EPS) * _silu(gate)).astype(ob_ref.dtype)

    def hg_task(n, h):
        base = HG_OFF + h * HEAD
        qc = proj_ref[n, :, base:base + HEAD]
        zf = proj_ref[n, :, base + MIX:base + MIX + HEAD]
        vc = proj_ref[n, :, base + 2 * MIX:base + 2 * MIX + HEAD]
        lb_h = lb_row[:, h * HEAD:(h + 1) * HEAD]
        s_neg = jax.nn.sigmoid(-zf)
        f_gate = jax.nn.sigmoid(zf) + lb_h * s_neg
        log_f = jnp.log(jnp.maximum(f_gate, F_TINY))
        kc = (1.0 - lb_h) * s_neg
        if padded:
            log_f = jnp.where(valid, log_f, 0.0)
            kc = jnp.where(valid, kc, 0.0)
            vc = jnp.where(valid, vc, 0.0)
        A = _cumsum_rows(ltri, log_f)
        yield
        hgs = hg_ref.at[n, h]
        hgs[0] = A
        hgs[1] = kc
        S = sh_ref[n, h]
        a_last = hgs[0, C - 1:C, :]
        q_s = _dot(qc * jnp.exp(A), S, 'nn', passes)
        s_upd = _dot(kc * jnp.exp(a_last - A), vc, 'tn', passes)
        score_rows = []
        for blk in range(C // SB):
            r0 = blk * SB
            qI = qc[r0:r0 + SB]
            AI = A[r0:r0 + SB]
            rowg = r0 + lax.broadcasted_iota(jnp.int32, (SB, C), 0)
            colg = lax.broadcasted_iota(jnp.int32, (SB, C), 1)
            cross = None
            if r0 > 0:
                a_b = hgs[0, r0 - 1:r0, :]
                kt = jnp.concatenate([kc[:r0] * jnp.exp(a_b - A[:r0]), jnp.zeros((C - r0, HEAD), F32)], axis=0)
                qt = qI * jnp.exp(AI - a_b)
                cross = _dot(qt, kt, 'nt', passes)
            sc_blk = jnp.zeros((SB, C), F32)
            for s in range(SB):
                a_s = hgs[0, r0 + s:r0 + s + 1, :]
                k_s = hgs[1, r0 + s:r0 + s + 1, :]
                w = jnp.exp(AI - a_s)
                val = jnp.sum(qI * k_s * w, axis=-1, keepdims=True)
                sc_blk = jnp.where(colg == r0 + s, val, sc_blk)
            yield
            if cross is not None:
                sc_blk = sc_blk + cross
            score_rows.append(jnp.where(rowg >= colg, sc_blk, 0.0))
        scores = score_rows[0] if len(score_rows) == 1 else jnp.concatenate(score_rows, axis=0)
        intra = _dot(scores, vc, 'nn', passes)
        dec_col = jnp.sum(jnp.where(eye128, jnp.exp(a_last), 0.0), axis=1, keepdims=True)
        sh_ref[n, h] = dec_col * S + s_upd
        yield
        o = intra + q_s
        gate = proj_ref[n, :, base + 3 * MIX:base + 3 * MIX + HEAD]
        ob_ref[n, :, 2 * MIX + h * HEAD:2 * MIX + (h + 1) * HEAD] = (
            (_rms(o) * hn) * jax.nn.sigmoid(gate)).astype(ob_ref.dtype)

    gdn_tasks, other_tasks = [], []
    for n in range(nb):
        qkv = None
        if not pre_conv:
            xpad_ref[n, SUBLANES:SUBLANES + C, :] = proj_ref[n, :, GDN_OFF:GDN_OFF + CONV_DIM]
            conv = xpad_ref[n, SUBLANES:SUBLANES + C, :] * cw_ref[CONV_W - 1:CONV_W, :]
            for j in range(CONV_W - 1):
                off = SUBLANES - (CONV_W - 1) + j
                conv = conv + xpad_ref[n, off:off + C, :] * cw_ref[j:j + 1, :]
            xpad_ref[n, 0:SUBLANES, :] = xpad_ref[n, C:C + SUBLANES, :]
            qkv = _silu(conv)
        ab = ab_ref[n]
        g_all = neg_a * jax.nn.softplus(ab + dt_b)
        beta_all = jax.nn.sigmoid(ab)
        if padded:
            g_all = jnp.where(valid, g_all, 0.0)
            beta_all = jnp.where(valid, beta_all, 0.0)
        G_all = _cumsum_rows(ltri, g_all)
        for h in range(N_HEADS):
            gdn_tasks.append(gdn_task(n, h, qkv, G_all, beta_all))
            other_tasks.append(hg_task(n, h))
            other_tasks.append(ret_task(n, h))
    tasks = gdn_tasks + other_tasks
    while tasks:
        alive = []
        for t in tasks:
            try:
                next(t)
                alive.append(t)
            except StopIteration:
                pass
        tasks = alive


def _mixer_call(proj3, ab3, states, prev_states, conv_w, head_params, gdn_norm, hg_norm, hgrn_lb,
                cosf, sinf, dtab, inner, toend, cdec, *, C, Lv, nb, layer, passes, pre_conv):
    B, Lp, _ = proj3.shape
    n_chunks = Lp // C
    depth = hgrn_lb.shape[0]
    n_prev = len(prev_states)
    in_state_spec = pl.BlockSpec((None, nb, N_HEADS, HEAD, HEAD), lambda b, i: (layer, b, 0, 0, 0))
    layer_state_spec = pl.BlockSpec((nb, N_HEADS, HEAD, HEAD), lambda b, i: (b, 0, 0, 0))
    if n_prev:
        out_state_spec = pl.BlockSpec((n_prev + 1, nb, N_HEADS, HEAD, HEAD), lambda b, i: (0, b, 0, 0, 0))
        state_shape = jax.ShapeDtypeStruct((n_prev + 1, B, N_HEADS, HEAD, HEAD), F32)
    else:
        out_state_spec = layer_state_spec
        state_shape = jax.ShapeDtypeStruct((B, N_HEADS, HEAD, HEAD), F32)
    ob_dtype = BF16 if C % (2 * SUBLANES) == 0 else F32

    def whole(shape):
        return pl.BlockSpec(shape, lambda b, i: (0,) * len(shape))

    has_state = states is not None
    operands = [proj3, ab3]
    in_specs = [
        pl.BlockSpec((nb, C, MIXER_COLS), lambda b, i: (b, i, 0)),
        pl.BlockSpec((nb, C, AB_COLS), lambda b, i: (b, i, 0)),
    ]
    if has_state:
        operands += [states[1], states[0], states[2], states[3]]
        in_specs += [pl.BlockSpec((None, nb, CONV_W - 1, CONV_DIM), lambda b, i: (layer, b, 0, 0)),
                     in_state_spec, in_state_spec, in_state_spec]
    operands += [conv_w, head_params, gdn_norm.reshape(1, HEAD), hg_norm.reshape(1, HEAD), hgrn_lb,
                 cosf, sinf, dtab, inner, toend, cdec]

    prev_flat = [a for group in prev_states for a in group]
    kern = functools.partial(_mixer_kernel, C=C, Lv=Lv, nb=nb, layer=layer, passes=passes,
                             has_state=has_state, n_prev=n_prev, pre_conv=pre_conv)
    scratch = [pltpu.VMEM((nb, N_HEADS, 2, C, HEAD), F32)]
    if not pre_conv:
        scratch.append(pltpu.VMEM((nb, C + SUBLANES, CONV_DIM), F32))
    return pl.pallas_call(
        kern,
        grid=(B // nb, n_chunks),
        in_specs=in_specs + [
            whole((CONV_W, CONV_DIM)),
            whole((2, AB_COLS)),
            whole((1, HEAD)),
            whole((1, HEAD)),
            whole((depth, MIX)),
            pl.BlockSpec((C, HEAD), lambda b, i: (i, 0)),
            pl.BlockSpec((C, HEAD), lambda b, i: (i, 0)),
            whole((N_HEADS, C, C)),
            whole((N_HEADS, C, HEAD)),
            whole((N_HEADS, C, HEAD)),
            whole((N_HEADS, 1, HEAD)),
        ] + [layer_state_spec] * len(prev_flat),
        out_specs=[
            pl.BlockSpec((nb, C, N_BRANCH * MIX), lambda b, i: (b, i, 0)),
            out_state_spec, out_state_spec, out_state_spec,
        ],
        out_shape=[
            jax.ShapeDtypeStruct((B, Lp, N_BRANCH * MIX), ob_dtype),
            state_shape, state_shape, state_shape,
        ],
        scratch_shapes=scratch,
        compiler_params=pltpu.CompilerParams(
            dimension_semantics=("arbitrary", "arbitrary"), vmem_limit_bytes=VMEM_LIMIT),
        name="mixer",
    )(*operands, *prev_flat)


def _post_kernel(ob_ref, mg_ref, x_ref, gt_ref, wb_ref, wo_ref, o_ref):
    merged = None
    for n in range(N_BRANCH):
        br = _dot(ob_ref[:, n * MIX:(n + 1) * MIX], wb_ref[n])
        term = mg_ref[:, n * D_MODEL:(n + 1) * D_MODEL].astype(F32) * br
        merged = term if merged is None else merged + term
    o_ref[...] = x_ref[...] + gt_ref[...] * _dot(merged, wo_ref[...])


def _post_call(ob2, gates, x2, mod, per_token, seq_len, w_branch, w_out):
    T = x2.shape[0]
    tm = _row_tile(T if per_token else seq_len, 512)
    tps = max(seq_len // tm, 1)
    return pl.pallas_call(
        _post_kernel,
        grid=(T // tm,),
        in_specs=[
            pl.BlockSpec((tm, N_BRANCH * MIX), lambda i: (i, 0)),
            pl.BlockSpec((tm, N_BRANCH * D_MODEL), lambda i: (i, 0)),
            pl.BlockSpec((tm, D_MODEL), lambda i: (i, 0)),
            _mod_spec(per_token, tm, tps, 2, 1),
            pl.BlockSpec((N_BRANCH, MIX, D_MODEL), lambda i: (0, 0, 0)),
            pl.BlockSpec((D_MODEL, D_MODEL), lambda i: (0, 0)),
        ],
        out_specs=pl.BlockSpec((tm, D_MODEL), lambda i: (i, 0)),
        out_shape=jax.ShapeDtypeStruct((T, D_MODEL), F32),
        compiler_params=pltpu.CompilerParams(
            dimension_semantics=("arbitrary",), vmem_limit_bytes=VMEM_LIMIT),
        name="post",
    )(ob2, gates, x2, mod, w_branch, w_out)


FF_SPLITS = (0, 1024, 2048, D_FF)


def _ffn_kernel(x_ref, sh_ref, sc_ref, gt_ref, nw_ref, fw_ref, wg_ref, wu_ref, wd_ref, o_ref, *, final):
    x = x_ref[...]
    h = (_rms(x) * nw_ref[...] * (1.0 + sc_ref[...]) + sh_ref[...]).astype(BF16)
    y = None
    for lo, hi in zip(FF_SPLITS[:-1], FF_SPLITS[1:]):
        g = jnp.dot(h, wg_ref[:, lo:hi], preferred_element_type=F32)
        u = jnp.dot(h, wu_ref[:, lo:hi], preferred_element_type=F32)
        part = _dot(_silu(g) * u, wd_ref[lo:hi, :])
        y = part if y is None else y + part
    xn = x + gt_ref[...] * y
    if final:
        xn = _rms(xn) * fw_ref[...]
    o_ref[...] = xn


def _ffn_call(x2, mod, per_token, seq_len, norm_w, final_w, w_gate_up, w_down, final):
    T = x2.shape[0]
    tm = _row_tile(T if per_token else seq_len, 512)
    tps = max(seq_len // tm, 1)
    resident = pl.Buffered(1)
    return pl.pallas_call(
        functools.partial(_ffn_kernel, final=final),
        grid=(T // tm,),
        in_specs=[
            pl.BlockSpec((tm, D_MODEL), lambda i: (i, 0)),
            _mod_spec(per_token, tm, tps, 3, 1),
            _mod_spec(per_token, tm, tps, 4, 1),
            _mod_spec(per_token, tm, tps, 5, 1),
            pl.BlockSpec((1, D_MODEL), lambda i: (0, 0)),
            pl.BlockSpec((1, D_MODEL), lambda i: (0, 0)),
            pl.BlockSpec((D_MODEL, D_FF), lambda i: (0, 0), pipeline_mode=resident),
            pl.BlockSpec((D_MODEL, D_FF), lambda i: (0, 1), pipeline_mode=resident),
            pl.BlockSpec((D_FF, D_MODEL), lambda i: (0, 0), pipeline_mode=resident),
        ],
        out_specs=pl.BlockSpec((tm, D_MODEL), lambda i: (i, 0)),
        out_shape=jax.ShapeDtypeStruct((T, D_MODEL), F32),
        compiler_params=pltpu.CompilerParams(
            dimension_semantics=("arbitrary",), vmem_limit_bytes=VMEM_LIMIT),
        name="ffn",
    )(x2, mod, mod, mod, norm_w.reshape(1, D_MODEL), final_w.reshape(1, D_MODEL),
      w_gate_up, w_gate_up, w_down)


def _rope_tables(pos0, length, padded_len):
    half = HEAD // 2
    pos = pos0 + jnp.arange(padded_len, dtype=jnp.int32)
    inv = ROPE_BASE ** (-jnp.arange(half, dtype=F32) / half)
    ang = pos.astype(F32)[:, None] * inv[None, :]
    cos, sin = jnp.cos(ang), jnp.sin(ang)
    return jnp.concatenate([cos, cos], axis=-1), jnp.concatenate([-sin, sin], axis=-1)


def _retention_tables(c_real, c_pad):
    log_gamma = jnp.log1p(-jnp.exp2(-5.0 - jnp.arange(N_HEADS, dtype=F32)))
    idx = jnp.arange(c_real, dtype=F32)
    rel = idx[:, None] - idx[None, :]
    mask = rel >= 0
    lg = log_gamma[:, None, None]
    dtab = jnp.where(mask, jnp.exp(jnp.where(mask, lg * rel, 0.0)), 0.0)
    inner = jnp.exp(log_gamma[:, None] * (idx + 1.0))
    to_end = jnp.exp(log_gamma[:, None] * (c_real - 1.0 - idx))
    cdec = jnp.exp(log_gamma * c_real)
    p = c_pad - c_real
    dtab = jnp.pad(dtab, ((0, 0), (0, p), (0, p)))
    inner = jnp.broadcast_to(jnp.pad(inner, ((0, 0), (0, p)))[:, :, None], (N_HEADS, c_pad, HEAD))
    to_end = jnp.broadcast_to(jnp.pad(to_end, ((0, 0), (0, p)))[:, :, None], (N_HEADS, c_pad, HEAD))
    cdec = jnp.broadcast_to(cdec[:, None, None], (N_HEADS, 1, HEAD))
    return dtab, inner, to_end, cdec


def _trunk(x, mod_all, states, pos0, wts, nb, nb_last, passes):
    B, L, _ = x.shape
    depth = mod_all.shape[0]
    c_real = CHUNK if L % CHUNK == 0 else L
    assert L % c_real == 0 and c_real <= CHUNK
    C = -(-c_real // SUBLANES) * SUBLANES
    Lp = L if C == c_real else C
    assert Lp == L or L == c_real
    per_token = Lp < 256
    fuse_conv = states is None and not per_token
    if Lp != L:
        x = jnp.pad(x, ((0, 0), (0, Lp - L), (0, 0)))
    T = B * Lp
    x2 = x.reshape(T, D_MODEL)
    cosf, sinf = _rope_tables(pos0, L, Lp)
    dtab, inner, toend, cdec = _retention_tables(c_real, C)

    tails, layer_states = [], []
    for l in range(depth):
        last = l == depth - 1
        if per_token:
            mod = jnp.repeat(mod_all[l], Lp, axis=0)
        else:
            mod = mod_all[l].reshape(B, 1, 6 * D_MODEL)
        proj, gates, ab, *raw_tail = _in_call(x2, mod, per_token, Lp, wts['norm_mix'][l], wts['w_main'][l],
                                              wts['w_ab'][l], wts['conv_w'][l], fuse_conv)
        proj3 = proj.reshape(B, Lp, MIXER_COLS)
        ob, sg, sr, sh = _mixer_call(
            proj3, ab.reshape(B, Lp, AB_COLS), states, layer_states if last else [],
            wts['conv_w'][l], wts['head_params'][l], wts['gdn_norm'][l], wts['hgrn_norm'][l], wts['hgrn_lb'],
            cosf, sinf, dtab, inner, toend, cdec, C=C, Lv=c_real, nb=nb_last if last else nb, layer=l,
            passes=passes, pre_conv=fuse_conv)
        if fuse_conv:
            tail = raw_tail[0].reshape(B, -1, SUBLANES, CONV_DIM)[:, -1, SUBLANES - (CONV_W - 1):]
        else:
            tail = proj3[:, L - min(L, CONV_W - 1):L, GDN_OFF:GDN_OFF + CONV_DIM]
            if L < CONV_W - 1:
                tail = jnp.concatenate([states[1][l][:, L:], tail], axis=1)
        tails.append(tail)
        layer_states.append((sg, sr, sh))
        x1 = _post_call(ob.reshape(T, N_BRANCH * MIX), gates, x2, mod, per_token, Lp,
                        wts['w_branch'][l], wts['w_out'][l])
        x2 = _ffn_call(x1, mod, per_token, Lp, wts['norm_ffn'][l], wts['final_norm'],
                       wts['w_gate_up'][l], wts['w_down'][l], final=(l == depth - 1))
    y = x2.reshape(B, Lp, D_MODEL)[:, :L]
    sg, sr, sh = layer_states[-1]
    if depth == 1:
        sg, sr, sh = sg[None], sr[None], sh[None]
    return y, (sg, jnp.stack(tails), sr, sh)


def _prep_weights(w_in, conv_w, gdn_a_log, gdn_dt_bias, gdn_norm, hgrn_lb, hgrn_norm, w_branch, w_out,
                  norm_mix, norm_ffn, w_gate_up, w_down, final_norm):
    ab0 = 4 * MIX
    skip = 2 * N_HEADS
    tiles = []
    for t in range(IN_TILES):
        lo, hi = t * IN_TN, (t + 1) * IN_TN
        pieces = []
        if lo < ab0:
            pieces.append(w_in[:, :, lo:min(hi, ab0)])
        if hi > ab0:
            pieces.append(w_in[:, :, max(lo, ab0) + skip:hi + skip])
        tiles.append(pieces[0] if len(pieces) == 1 else jnp.concatenate(pieces, axis=-1))
    w_main = jnp.stack(tiles, axis=1).astype(BF16)
    w_ab = jnp.pad(w_in[:, :, ab0:ab0 + 2 * N_HEADS],
                   ((0, 0), (0, 0), (0, AB_COLS - 2 * N_HEADS))).astype(BF16)
    head_params = jnp.pad(jnp.stack([gdn_a_log, gdn_dt_bias], axis=1),
                          ((0, 0), (0, 0), (0, AB_COLS - N_HEADS))).astype(F32)
    return dict(
        w_main=w_main, w_ab=w_ab, conv_w=conv_w, head_params=head_params, gdn_norm=gdn_norm,
        hgrn_norm=hgrn_norm, hgrn_lb=hgrn_lb, w_branch=w_branch.astype(BF16), w_out=w_out.astype(BF16),
        norm_mix=norm_mix, norm_ffn=norm_ffn, w_gate_up=w_gate_up.astype(BF16),
        w_down=w_down.astype(BF16), final_norm=final_norm)


def kernel(x_prompt, x_sample, state_gdn, state_gdn_conv, state_ret, state_hgrn, c_prompt, c_sample,
           w_in, conv_w, gdn_a_log, gdn_dt_bias, gdn_norm, hgrn_lb, hgrn_norm, w_branch, w_out,
           w_ada, b_ada, norm_mix, norm_ffn, w_gate_up, w_down, final_norm):
    Bp = x_prompt.shape[0]
    wts = _prep_weights(w_in, conv_w, gdn_a_log, gdn_dt_bias, gdn_norm, hgrn_lb, hgrn_norm, w_branch, w_out,
                        norm_mix, norm_ffn, w_gate_up, w_down, final_norm)
    c_all = jnp.concatenate([c_prompt, c_sample], axis=0)
    mod_all = _ada_call(c_all, w_ada, b_ada)
    y_p, st_p = _trunk(x_prompt, mod_all[:, :Bp], None, 0, wts, nb=2, nb_last=2, passes=1)
    y_s, st_s = _trunk(x_sample, mod_all[:, Bp:], (state_gdn, state_gdn_conv, state_ret, state_hgrn),
                       PAST_LEN, wts, nb=8, nb_last=4, passes=1)
    return (y_p, y_s) + st_p + st_s
```

```python
import functools

import jax
import jax.numpy as jnp
from jax import lax
from jax.experimental import pallas as pl
from jax.experimental.pallas import tpu as pltpu

F32 = jnp.float32
BF16 = jnp.bfloat16

D_MODEL = 1024
N_HEADS = 4
HEAD = 128
MIX = N_HEADS * HEAD
CONV_W = 4
CONV_DIM = 3 * MIX
CHUNK = 64
N_BRANCH = 3
D_FF = 2816
ROPE_BASE = 10000.0
EPS = 1e-6
F_TINY = 1e-30
PAST_LEN = 16384

IN_TN = 3 * MIX
MIXER_LAYOUT = (
    ('gdn_q', 0, 'conv'), ('gdn_z', 3 * MIX, 'silu'), ('ret_q', 4 * MIX + 8, None),
    ('gdn_k', MIX, 'conv'), ('ret_k', 5 * MIX + 8, None), ('ret_v', 6 * MIX + 8, None),
    ('gdn_v', 2 * MIX, 'conv'), ('ret_g', 7 * MIX + 8, 'silu'), ('hg_q', 8 * MIX + 8, None),
    ('hg_f', 9 * MIX + 8, None), ('hg_i', 10 * MIX + 8, None), ('hg_g', 11 * MIX + 8, 'sigmoid'),
)
COL = {name: idx * MIX for idx, (name, _, _) in enumerate(MIXER_LAYOUT)}
MIXER_COLS = len(MIXER_LAYOUT) * MIX
MERGE_SRC = 12 * MIX + 8
MAIN_COLS = MIXER_COLS + N_BRANCH * D_MODEL
AB_COLS = 128
AB_SRC = 4 * MIX
IN_TILES = MAIN_COLS // IN_TN
MIXER_TILES = MIXER_COLS // IN_TN
CONV_GROUPS = ('gdn_q', 'gdn_k', 'gdn_v')

SUBLANES = 8
STAGGER_ROUNDS = 4
VMEM_LIMIT = 56 * 1024 * 1024

_DN = {
    'nn': (((1,), (0,)), ((), ())),
    'nt': (((1,), (1,)), ((), ())),
    'tn': (((0,), (0,)), ((), ())),
}


def _split2(a):
    hi = a.astype(BF16)
    lo = (a - hi.astype(F32)).astype(BF16)
    return hi, lo


def _dot(a, b, dims='nn', passes=1):
    dn = _DN[dims]
    if passes == 1:
        return lax.dot_general(a.astype(BF16), b.astype(BF16), dn, preferred_element_type=F32)
    a_hi, a_lo = _split2(a)
    b_hi, b_lo = _split2(b)
    out = lax.dot_general(a_hi, b_lo, dn, preferred_element_type=F32)
    out = out + lax.dot_general(a_lo, b_hi, dn, preferred_element_type=F32)
    return out + lax.dot_general(a_hi, b_hi, dn, preferred_element_type=F32)


def _cumsum_rows(ltri_bf16, x):
    x1 = x.astype(BF16)
    r1 = x - x1.astype(F32)
    x2 = r1.astype(BF16)
    x3 = (r1 - x2.astype(F32)).astype(BF16)
    dn = _DN['nn']
    out = lax.dot_general(ltri_bf16, x3, dn, preferred_element_type=F32)
    out = out + lax.dot_general(ltri_bf16, x2, dn, preferred_element_type=F32)
    return out + lax.dot_general(ltri_bf16, x1, dn, preferred_element_type=F32)


def _silu(x):
    return x * jax.nn.sigmoid(x)


def _rms(x):
    return x * lax.rsqrt(jnp.mean(x * x, axis=-1, keepdims=True) + EPS)


def _ada_kernel(c_ref, w_ref, b_ref, o_ref):
    cs = _silu(c_ref[...])
    o_ref[...] = _dot(cs, w_ref[...]) + b_ref[...]


def _ada_call(c_all, w_ada, b_ada):
    depth = w_ada.shape[0]
    rows = c_all.shape[0]
    n_out = w_ada.shape[2]
    tn = 1536
    return pl.pallas_call(
        _ada_kernel,
        grid=(depth, n_out // tn),
        in_specs=[
            pl.BlockSpec((rows, D_MODEL), lambda l, j: (0, 0)),
            pl.BlockSpec((None, D_MODEL, tn), lambda l, j: (l, 0, j)),
            pl.BlockSpec((None, 1, tn), lambda l, j: (l, 0, j)),
        ],
        out_specs=pl.BlockSpec((None, rows, tn), lambda l, j: (l, 0, j)),
        out_shape=jax.ShapeDtypeStruct((depth, rows, n_out), F32),
        compiler_params=pltpu.CompilerParams(
            dimension_semantics=("arbitrary", "arbitrary"), vmem_limit_bytes=VMEM_LIMIT),
        name="ada",
    )(c_all, w_ada, b_ada.reshape(depth, 1, n_out))


def _row_tile(n_rows, preferred):
    tm = preferred
    while n_rows % tm:
        tm //= 2
    return tm


def _mod_spec(per_token, tm, tiles_per_seq, seg, ngrid):
    if per_token:
        if ngrid == 2:
            return pl.BlockSpec((tm, D_MODEL), lambda i, j: (i, seg))
        return pl.BlockSpec((tm, D_MODEL), lambda i: (i, seg))
    if ngrid == 2:
        return pl.BlockSpec((None, 1, D_MODEL), lambda i, j: (i // tiles_per_seq, 0, seg))
    return pl.BlockSpec((None, 1, D_MODEL), lambda i: (i // tiles_per_seq, 0, seg))


def _in_kernel(*refs, fuse_conv, tiles_per_seq):
    if fuse_conv:
        (x_ref, sh_ref, sc_ref, nw_ref, w_ref, wab_ref, cw_ref,
         o_ref, g_ref, ab_ref, tail_ref, h_scr, xp_scr, carry_scr) = refs
    else:
        x_ref, sh_ref, sc_ref, nw_ref, w_ref, wab_ref, o_ref, g_ref, ab_ref, h_scr = refs
    i = pl.program_id(0)
    j = pl.program_id(1)
    tm = x_ref.shape[0]

    def conv_group(c, raw):
        lo = c * MIX
        xp_scr[0:SUBLANES, :] = carry_scr[:, lo:lo + MIX]
        xp_scr[SUBLANES:SUBLANES + tm, :] = raw
        conv = xp_scr[SUBLANES:SUBLANES + tm, :] * cw_ref[CONV_W - 1:CONV_W, lo:lo + MIX]
        for t in range(CONV_W - 1):
            off = SUBLANES - (CONV_W - 1) + t
            conv = conv + xp_scr[off:off + tm, :] * cw_ref[t:t + 1, lo:lo + MIX]
        carry_scr[:, lo:lo + MIX] = xp_scr[tm:tm + SUBLANES, :]
        tail_ref[:, lo:lo + MIX] = xp_scr[tm:tm + SUBLANES, :]
        act = _silu(conv)
        if c == 2:
            o_ref[:, 0:MIX] = act
            return
        for hh in range(N_HEADS):
            a = act[:, hh * HEAD:(hh + 1) * HEAD]
            a = a * lax.rsqrt(jnp.sum(a * a, axis=-1, keepdims=True) + EPS)
            if c == 0:
                a = a * (HEAD ** -0.5)
            o_ref[:, hh * HEAD:(hh + 1) * HEAD] = a

    def tile(t, hb):
        for g in range(3):
            epilogue = MIXER_LAYOUT[3 * t + g][2]
            out = jnp.dot(hb, w_ref[t, :, g * MIX:(g + 1) * MIX], preferred_element_type=F32)
            if epilogue == 'conv' and fuse_conv:
                conv_group(t, out)
                continue
            if epilogue == 'silu':
                out = _silu(out)
            elif epilogue == 'sigmoid':
                out = jax.nn.sigmoid(out)
            o_ref[:, g * MIX:(g + 1) * MIX] = out

    @pl.when(j == 0)
    def _():
        h = _rms(x_ref[...]) * nw_ref[...]
        h = h * (1.0 + sc_ref[...]) + sh_ref[...]
        hb = h.astype(BF16)
        h_scr[...] = hb
        ab_ref[...] = jnp.dot(hb, wab_ref[...], preferred_element_type=F32)
        if fuse_conv:
            @pl.when(i % tiles_per_seq == 0)
            def _():
                carry_scr[...] = jnp.zeros_like(carry_scr)
        tile(0, hb)

    for t in range(1, MIXER_TILES):
        @pl.when(j == t)
        def _(t=t):
            tile(t, h_scr[...])

    @pl.when(j >= MIXER_TILES)
    def _():
        acc = jnp.dot(h_scr[...], w_ref[j], preferred_element_type=F32)
        g_ref[...] = jax.nn.sigmoid(acc).astype(BF16)


def _in_call(x2, mod, per_token, seq_len, norm_w, w_main, w_ab, conv_w, fuse_conv):
    T = x2.shape[0]
    tm = _row_tile(T if per_token else seq_len, 512 if per_token else 1024)
    tn = IN_TN
    tps = max(seq_len // tm, 1)
    last_gate = IN_TILES - MIXER_TILES - 1
    assert not fuse_conv or not per_token
    assert all(MIXER_LAYOUT[3 * c][0] == name for c, name in enumerate(CONV_GROUPS))

    def gate_block(i, j):
        on_gate = j >= MIXER_TILES
        row = jnp.where(on_gate, i, jnp.maximum(i - 1, 0))
        col = jnp.where(on_gate, j - MIXER_TILES, jnp.where(i > 0, last_gate, 0))
        return row, col

    in_specs = [
        pl.BlockSpec((tm, D_MODEL), lambda i, j: (i, 0)),
        _mod_spec(per_token, tm, tps, 0, 2),
        _mod_spec(per_token, tm, tps, 1, 2),
        pl.BlockSpec((1, D_MODEL), lambda i, j: (0, 0)),
        pl.BlockSpec((IN_TILES, D_MODEL, tn), lambda i, j: (0, 0, 0), pipeline_mode=pl.Buffered(1)),
        pl.BlockSpec((D_MODEL, AB_COLS), lambda i, j: (0, 0)),
    ]
    out_specs = [
        pl.BlockSpec((tm, tn), lambda i, j: (i, jnp.minimum(j, MIXER_TILES - 1))),
        pl.BlockSpec((tm, tn), gate_block),
        pl.BlockSpec((tm, AB_COLS), lambda i, j: (i, 0)),
    ]
    out_shape = [
        jax.ShapeDtypeStruct((T, MIXER_COLS), F32),
        jax.ShapeDtypeStruct((T, N_BRANCH * D_MODEL), BF16),
        jax.ShapeDtypeStruct((T, AB_COLS), F32),
    ]
    scratch = [pltpu.VMEM((tm, D_MODEL), BF16)]
    operands = [x2, mod, mod, norm_w.reshape(1, D_MODEL), w_main, w_ab]
    if fuse_conv:
        in_specs.append(pl.BlockSpec((CONV_W, CONV_DIM), lambda i, j: (0, 0)))
        operands.append(conv_w)
        out_specs.append(pl.BlockSpec((SUBLANES, CONV_DIM), lambda i, j: (i, 0)))
        out_shape.append(jax.ShapeDtypeStruct((T // tm * SUBLANES, CONV_DIM), F32))
        scratch += [pltpu.VMEM((tm + SUBLANES, MIX), F32), pltpu.VMEM((SUBLANES, CONV_DIM), F32)]
    return pl.pallas_call(
        functools.partial(_in_kernel, fuse_conv=fuse_conv, tiles_per_seq=tps),
        grid=(T // tm, IN_TILES),
        in_specs=in_specs,
        out_specs=out_specs,
        out_shape=out_shape,
        scratch_shapes=scratch,
        compiler_params=pltpu.CompilerParams(
            dimension_semantics=("arbitrary", "arbitrary"), vmem_limit_bytes=VMEM_LIMIT),
        name="in_proj",
    )(*operands)


def _mixer_kernel(*refs, C, Lv, nb, layer, passes, has_state, n_prev, pre_conv):
    proj_ref, ab_ref = refs[:2]
    pos = 2
    if has_state:
        conv0_ref, sg0_ref, sr0_ref, sh0_ref = refs[pos:pos + 4]
        pos += 4
    (cw_ref, hp_ref, gn_ref, hn_ref, lb_ref, cos_ref, sin_ref,
     dtab_ref, inner_ref, toend_ref, cdec_ref) = refs[pos:pos + 11]
    pos += 11
    prev_refs = refs[pos:pos + 3 * n_prev]
    ob_ref, sg_ref, sr_ref, sh_ref, hg_ref = refs[pos + 3 * n_prev:pos + 3 * n_prev + 5]
    xpad_ref = None if pre_conv else refs[-1]
    if n_prev:
        stacked = (sg_ref, sr_ref, sh_ref)
        sg_ref, sr_ref, sh_ref = (r.at[n_prev] for r in stacked)
    tail0 = SUBLANES - (CONV_W - 1)

    @pl.when(pl.program_id(1) == 0)
    def _():
        for p in range(n_prev):
            for k in range(3):
                stacked[k][p] = prev_refs[3 * p + k][...]
        if has_state:
            sg_ref[...] = sg0_ref[...]
            sr_ref[...] = sr0_ref[...]
            sh_ref[...] = sh0_ref[...]
            xpad_ref[:, tail0:SUBLANES, :] = conv0_ref[...]
        else:
            sg_ref[...] = jnp.zeros_like(sg_ref)
            sr_ref[...] = jnp.zeros_like(sr_ref)
            sh_ref[...] = jnp.zeros_like(sh_ref)
            if not pre_conv:
                xpad_ref[:, tail0:SUBLANES, :] = jnp.zeros((nb, CONV_W - 1, CONV_DIM), F32)

    padded = Lv < C
    row1 = lax.broadcasted_iota(jnp.int32, (C, 1), 0)
    valid = row1 < Lv
    ri = lax.broadcasted_iota(jnp.int32, (C, C), 0)
    ci = lax.broadcasted_iota(jnp.int32, (C, C), 1)
    incl = ri >= ci
    strict = ri > ci
    eye = ri == ci
    ltri = jnp.where(incl, 1.0, 0.0).astype(BF16)
    r128 = lax.broadcasted_iota(jnp.int32, (HEAD, HEAD), 0)
    c128 = lax.broadcasted_iota(jnp.int32, (HEAD, HEAD), 1)
    eye128 = r128 == c128
    n_levels = max((Lv - 1).bit_length(), 1)

    lbp = lb_ref[...]
    lbe = jnp.exp(lbp - jnp.max(lbp, axis=0, keepdims=True))
    lbw = lbe / jnp.sum(lbe, axis=0, keepdims=True)
    lb_row = lbw[0:1, :]
    for d in range(1, layer + 1):
        lb_row = lb_row + lbw[d:d + 1, :]
    lb_row = lb_row - lbw[0:1, :]

    neg_a = -jnp.exp(hp_ref[0:1, :])
    dt_b = hp_ref[1:2, :]
    gn = gn_ref[...]
    hn = hn_ref[...]
    cosf = cos_ref[...]
    sinf = sin_ref[...]

    SB = SUBLANES

    def head_cols(n, name, h):
        return proj_ref[n, :, COL[name] + h * HEAD:COL[name] + (h + 1) * HEAD]

    def gdn_task(n, h, qkv, G_all, beta_all):
        if pre_conv:
            q = head_cols(n, 'gdn_q', h)
            k = head_cols(n, 'gdn_k', h)
            v = head_cols(n, 'gdn_v', h)
        else:
            q = qkv[:, h * HEAD:(h + 1) * HEAD]
            k = qkv[:, MIX + h * HEAD:MIX + (h + 1) * HEAD]
            v = qkv[:, 2 * MIX + h * HEAD:2 * MIX + (h + 1) * HEAD]
            q = q * lax.rsqrt(jnp.sum(q * q, axis=-1, keepdims=True) + EPS) * (HEAD ** -0.5)
            k = k * lax.rsqrt(jnp.sum(k * k, axis=-1, keepdims=True) + EPS)
        kk = _dot(k, k, 'nt', passes)
        qk = _dot(q, k, 'nt', passes)
        yield
        Gc = G_all[:, h:h + 1]
        beta = beta_all[:, N_HEADS + h:N_HEADS + h + 1]
        Gr = jnp.sum(jnp.where(eye, Gc, 0.0), axis=0, keepdims=True)
        G_last = G_all[C - 1:C, h:h + 1]
        eG = jnp.exp(Gc)
        decay = jnp.where(incl, jnp.exp(jnp.where(incl, Gc - Gr, 0.0)), 0.0)
        P = -jnp.where(strict, beta * kk * decay, 0.0)
        sol = jnp.concatenate([beta * v, (beta * eG) * k], axis=-1)
        for lvl in range(n_levels):
            upd = _dot(P, sol, 'nn', passes)
            if lvl + 1 < n_levels:
                P = _dot(P, P, 'nn', passes)
            yield
            sol = sol + upd
        S = sg_ref[n, h]
        sk_s = _dot(sol[:, HEAD:], S, 'nn', passes)
        q_s = _dot(q, S, 'nn', passes)
        yield
        u = sol[:, :HEAD] - sk_s
        k_end = k * jnp.exp(G_last - Gc)
        intra = _dot(qk * decay, u, 'nn', passes)
        s_upd = _dot(k_end, u, 'tn', passes)
        yield
        o = eG * q_s + intra
        sg_ref[n, h] = jnp.exp(G_last) * S + s_upd
        z_act = head_cols(n, 'gdn_z', h)
        ob_ref[n, :, h * HEAD:(h + 1) * HEAD] = ((_rms(o) * gn) * z_act).astype(ob_ref.dtype)

    def ret_task(n, h):
        q = head_cols(n, 'ret_q', h)
        k = head_cols(n, 'ret_k', h)
        v = head_cols(n, 'ret_v', h)
        q = q * cosf + pltpu.roll(q, HEAD // 2, 1) * sinf
        k = (k * cosf + pltpu.roll(k, HEAD // 2, 1) * sinf) * (HEAD ** -0.5)
        if padded:
            v = jnp.where(valid, v, 0.0)
        S = sr_ref[n, h]
        qk = _dot(q, k, 'nt', passes)
        q_s = _dot(q, S, 'nn', passes)
        s_upd = _dot(k * toend_ref[h], v, 'tn', passes)
        yield
        intra = _dot(qk * dtab_ref[h], v, 'nn', passes)
        sr_ref[n, h] = cdec_ref[h] * S + s_upd
        yield
        o = intra + inner_ref[h] * q_s
        g_act = head_cols(n, 'ret_g', h)
        mu = jnp.mean(o, axis=-1, keepdims=True)
        oc = o - mu
        var = jnp.mean(oc * oc, axis=-1, keepdims=True)
        ob_ref[n, :, MIX + h * HEAD:MIX + (h + 1) * HEAD] = (
            oc * lax.rsqrt(var + EPS) * g_act).astype(ob_ref.dtype)

    def hg_task(n, h):
        qc = head_cols(n, 'hg_q', h)
        zf = head_cols(n, 'hg_f', h)
        vc = head_cols(n, 'hg_i', h)
        lb_h = lb_row[:, h * HEAD:(h + 1) * HEAD]
        s_neg = jax.nn.sigmoid(-zf)
        f_gate = jax.nn.sigmoid(zf) + lb_h * s_neg
        log_f = jnp.log(jnp.maximum(f_gate, F_TINY))
        kc = (1.0 - lb_h) * s_neg
        if padded:
            log_f = jnp.where(valid, log_f, 0.0)
            kc = jnp.where(valid, kc, 0.0)
            vc = jnp.where(valid, vc, 0.0)
        A = _cumsum_rows(ltri, log_f)
        yield
        hgs = hg_ref.at[n, h]
        hgs[0] = A
        hgs[1] = kc
        S = sh_ref[n, h]
        a_last = hgs[0, C - 1:C, :]
        q_s = _dot(qc * jnp.exp(A), S, 'nn', passes)
        s_upd = _dot(kc * jnp.exp(a_last - A), vc, 'tn', passes)
        score_rows = []
        for blk in range(C // SB):
            r0 = blk * SB
            qI = qc[r0:r0 + SB]
            AI = A[r0:r0 + SB]
            rowg = r0 + lax.broadcasted_iota(jnp.int32, (SB, C), 0)
            colg = lax.broadcasted_iota(jnp.int32, (SB, C), 1)
            cross = None
            if r0 > 0:
                a_b = hgs[0, r0 - 1:r0, :]
                kt = jnp.concatenate([kc[:r0] * jnp.exp(a_b - A[:r0]), jnp.zeros((C - r0, HEAD), F32)], axis=0)
                qt = qI * jnp.exp(AI - a_b)
                cross = _dot(qt, kt, 'nt', passes)
            sc_blk = jnp.zeros((SB, C), F32)
            for s in range(SB):
                a_s = hgs[0, r0 + s:r0 + s + 1, :]
                k_s = hgs[1, r0 + s:r0 + s + 1, :]
                w = jnp.exp(AI - a_s)
                val = jnp.sum(qI * k_s * w, axis=-1, keepdims=True)
                sc_blk = jnp.where(colg == r0 + s, val, sc_blk)
            yield
            if cross is not None:
                sc_blk = sc_blk + cross
            score_rows.append(jnp.where(rowg >= colg, sc_blk, 0.0))
        scores = score_rows[0] if len(score_rows) == 1 else jnp.concatenate(score_rows, axis=0)
        intra = _dot(scores, vc, 'nn', passes)
        dec_col = jnp.sum(jnp.where(eye128, jnp.exp(a_last), 0.0), axis=1, keepdims=True)
        sh_ref[n, h] = dec_col * S + s_upd
        yield
        o = intra + q_s
        g_act = head_cols(n, 'hg_g', h)
        ob_ref[n, :, 2 * MIX + h * HEAD:2 * MIX + (h + 1) * HEAD] = (
            (_rms(o) * hn) * g_act).astype(ob_ref.dtype)

    stagger = STAGGER_ROUNDS if nb <= 2 else 0
    gdn_tasks, other_tasks = [], []
    for n in range(nb):
        qkv = None
        if not pre_conv:
            for c, name in enumerate(CONV_GROUPS):
                xpad_ref[n, SUBLANES:SUBLANES + C, c * MIX:(c + 1) * MIX] = proj_ref[n, :, COL[name]:COL[name] + MIX]
            conv = xpad_ref[n, SUBLANES:SUBLANES + C, :] * cw_ref[CONV_W - 1:CONV_W, :]
            for j in range(CONV_W - 1):
                off = SUBLANES - (CONV_W - 1) + j
                conv = conv + xpad_ref[n, off:off + C, :] * cw_ref[j:j + 1, :]
            xpad_ref[n, 0:SUBLANES, :] = xpad_ref[n, C:C + SUBLANES, :]
            qkv = _silu(conv)
        ab = ab_ref[n]
        g_all = neg_a * jax.nn.softplus(ab + dt_b)
        beta_all = jax.nn.sigmoid(ab)
        if padded:
            g_all = jnp.where(valid, g_all, 0.0)
            beta_all = jnp.where(valid, beta_all, 0.0)
        G_all = _cumsum_rows(ltri, g_all)
        for h in range(N_HEADS):
            gdn_tasks.append([n * stagger, gdn_task(n, h, qkv, G_all, beta_all)])
            other_tasks.append([n * stagger, hg_task(n, h)])
            other_tasks.append([n * stagger, ret_task(n, h)])
    tasks = gdn_tasks + other_tasks
    while tasks:
        alive = []
        for t in tasks:
            if t[0] > 0:
                t[0] -= 1
                alive.append(t)
                continue
            try:
                next(t[1])
                alive.append(t)
            except StopIteration:
                pass
        tasks = alive


def _mixer_call(proj3, ab3, states, prev_states, conv_w, head_params, gdn_norm, hg_norm, hgrn_lb,
                cosf, sinf, dtab, inner, toend, cdec, *, C, Lv, nb, layer, passes, pre_conv):
    B, Lp, _ = proj3.shape
    n_chunks = Lp // C
    depth = hgrn_lb.shape[0]
    n_prev = len(prev_states)
    in_state_spec = pl.BlockSpec((None, nb, N_HEADS, HEAD, HEAD), lambda b, i: (layer, b, 0, 0, 0))
    layer_state_spec = pl.BlockSpec((nb, N_HEADS, HEAD, HEAD), lambda b, i: (b, 0, 0, 0))
    if n_prev:
        out_state_spec = pl.BlockSpec((n_prev + 1, nb, N_HEADS, HEAD, HEAD), lambda b, i: (0, b, 0, 0, 0))
        state_shape = jax.ShapeDtypeStruct((n_prev + 1, B, N_HEADS, HEAD, HEAD), F32)
    else:
        out_state_spec = layer_state_spec
        state_shape = jax.ShapeDtypeStruct((B, N_HEADS, HEAD, HEAD), F32)
    ob_dtype = BF16 if C % (2 * SUBLANES) == 0 else F32

    def whole(shape):
        return pl.BlockSpec(shape, lambda b, i: (0,) * len(shape))

    has_state = states is not None
    operands = [proj3, ab3]
    in_specs = [
        pl.BlockSpec((nb, C, MIXER_COLS), lambda b, i: (b, i, 0)),
        pl.BlockSpec((nb, C, AB_COLS), lambda b, i: (b, i, 0)),
    ]
    if has_state:
        operands += [states[1], states[0], states[2], states[3]]
        in_specs += [pl.BlockSpec((None, nb, CONV_W - 1, CONV_DIM), lambda b, i: (layer, b, 0, 0)),
                     in_state_spec, in_state_spec, in_state_spec]
    operands += [conv_w, head_params, gdn_norm.reshape(1, HEAD), hg_norm.reshape(1, HEAD), hgrn_lb,
                 cosf, sinf, dtab, inner, toend, cdec]

    prev_flat = [a for group in prev_states for a in group]
    kern = functools.partial(_mixer_kernel, C=C, Lv=Lv, nb=nb, layer=layer, passes=passes,
                             has_state=has_state, n_prev=n_prev, pre_conv=pre_conv)
    scratch = [pltpu.VMEM((nb, N_HEADS, 2, C, HEAD), F32)]
    if not pre_conv:
        scratch.append(pltpu.VMEM((nb, C + SUBLANES, CONV_DIM), F32))
    return pl.pallas_call(
        kern,
        grid=(B // nb, n_chunks),
        in_specs=in_specs + [
            whole((CONV_W, CONV_DIM)),
            whole((2, AB_COLS)),
            whole((1, HEAD)),
            whole((1, HEAD)),
            whole((depth, MIX)),
            pl.BlockSpec((C, HEAD), lambda b, i: (i, 0)),
            pl.BlockSpec((C, HEAD), lambda b, i: (i, 0)),
            whole((N_HEADS, C, C)),
            whole((N_HEADS, C, HEAD)),
            whole((N_HEADS, C, HEAD)),
            whole((N_HEADS, 1, HEAD)),
        ] + [layer_state_spec] * len(prev_flat),
        out_specs=[
            pl.BlockSpec((nb, C, N_BRANCH * MIX), lambda b, i: (b, i, 0)),
            out_state_spec, out_state_spec, out_state_spec,
        ],
        out_shape=[
            jax.ShapeDtypeStruct((B, Lp, N_BRANCH * MIX), ob_dtype),
            state_shape, state_shape, state_shape,
        ],
        scratch_shapes=scratch,
        compiler_params=pltpu.CompilerParams(
            dimension_semantics=("arbitrary", "arbitrary"), vmem_limit_bytes=VMEM_LIMIT),
        name="mixer",
    )(*operands, *prev_flat)


FF_SPLITS = (0, 1024, 2048, D_FF)


def _post_ffn_kernel(ob_ref, mg_ref, x_ref, gtm_ref, sh_ref, sc_ref, gtf_ref, nw_ref, fw_ref,
                     wb_ref, wo_ref, wg_ref, wu_ref, wd_ref, o_ref, *, final):
    merged = None
    for n in range(N_BRANCH):
        br = _dot(ob_ref[:, n * MIX:(n + 1) * MIX], wb_ref[n])
        term = mg_ref[:, n * D_MODEL:(n + 1) * D_MODEL].astype(F32) * br
        merged = term if merged is None else merged + term
    x = x_ref[...] + gtm_ref[...] * _dot(merged, wo_ref[...])
    h = (_rms(x) * nw_ref[...] * (1.0 + sc_ref[...]) + sh_ref[...]).astype(BF16)
    y = None
    for lo, hi in zip(FF_SPLITS[:-1], FF_SPLITS[1:]):
        g = jnp.dot(h, wg_ref[:, lo:hi], preferred_element_type=F32)
        u = jnp.dot(h, wu_ref[:, lo:hi], preferred_element_type=F32)
        part = _dot(_silu(g) * u, wd_ref[lo:hi, :])
        y = part if y is None else y + part
    xn = x + gtf_ref[...] * y
    if final:
        xn = _rms(xn) * fw_ref[...]
    o_ref[...] = xn


def _post_ffn_call(ob2, gates, x2, mod, per_token, seq_len, norm_w, final_w, w_branch, w_out, w_gate_up, w_down,
                   final):
    T = x2.shape[0]
    tm = _row_tile(T if per_token else seq_len, 256 if per_token else 512)
    tps = max(seq_len // tm, 1)
    resident = pl.Buffered(1)
    return pl.pallas_call(
        functools.partial(_post_ffn_kernel, final=final),
        grid=(T // tm,),
        in_specs=[
            pl.BlockSpec((tm, N_BRANCH * MIX), lambda i: (i, 0)),
            pl.BlockSpec((tm, N_BRANCH * D_MODEL), lambda i: (i, 0)),
            pl.BlockSpec((tm, D_MODEL), lambda i: (i, 0)),
            _mod_spec(per_token, tm, tps, 2, 1),
            _mod_spec(per_token, tm, tps, 3, 1),
            _mod_spec(per_token, tm, tps, 4, 1),
            _mod_spec(per_token, tm, tps, 5, 1),
            pl.BlockSpec((1, D_MODEL), lambda i: (0, 0)),
            pl.BlockSpec((1, D_MODEL), lambda i: (0, 0)),
            pl.BlockSpec((N_BRANCH, MIX, D_MODEL), lambda i: (0, 0, 0), pipeline_mode=resident),
            pl.BlockSpec((D_MODEL, D_MODEL), lambda i: (0, 0), pipeline_mode=resident),
            pl.BlockSpec((D_MODEL, D_FF), lambda i: (0, 0), pipeline_mode=resident),
            pl.BlockSpec((D_MODEL, D_FF), lambda i: (0, 1), pipeline_mode=resident),
            pl.BlockSpec((D_FF, D_MODEL), lambda i: (0, 0), pipeline_mode=resident),
        ],
        out_specs=pl.BlockSpec((tm, D_MODEL), lambda i: (i, 0)),
        out_shape=jax.ShapeDtypeStruct((T, D_MODEL), F32),
        compiler_params=pltpu.CompilerParams(
            dimension_semantics=("arbitrary",), vmem_limit_bytes=VMEM_LIMIT),
        name="post_ffn",
    )(ob2, gates, x2, mod, mod, mod, mod, norm_w.reshape(1, D_MODEL), final_w.reshape(1, D_MODEL),
      w_branch, w_out, w_gate_up, w_gate_up, w_down)


def _rope_tables(pos0, length, padded_len):
    half = HEAD // 2
    pos = pos0 + jnp.arange(padded_len, dtype=jnp.int32)
    inv = ROPE_BASE ** (-jnp.arange(half, dtype=F32) / half)
    ang = pos.astype(F32)[:, None] * inv[None, :]
    cos, sin = jnp.cos(ang), jnp.sin(ang)
    return jnp.concatenate([cos, cos], axis=-1), jnp.concatenate([-sin, sin], axis=-1)


def _retention_tables(c_real, c_pad):
    log_gamma = jnp.log1p(-jnp.exp2(-5.0 - jnp.arange(N_HEADS, dtype=F32)))
    idx = jnp.arange(c_real, dtype=F32)
    rel = idx[:, None] - idx[None, :]
    mask = rel >= 0
    lg = log_gamma[:, None, None]
    dtab = jnp.where(mask, jnp.exp(jnp.where(mask, lg * rel, 0.0)), 0.0)
    inner = jnp.exp(log_gamma[:, None] * (idx + 1.0))
    to_end = jnp.exp(log_gamma[:, None] * (c_real - 1.0 - idx))
    cdec = jnp.exp(log_gamma * c_real)
    p = c_pad - c_real
    dtab = jnp.pad(dtab, ((0, 0), (0, p), (0, p)))
    inner = jnp.broadcast_to(jnp.pad(inner, ((0, 0), (0, p)))[:, :, None], (N_HEADS, c_pad, HEAD))
    to_end = jnp.broadcast_to(jnp.pad(to_end, ((0, 0), (0, p)))[:, :, None], (N_HEADS, c_pad, HEAD))
    cdec = jnp.broadcast_to(cdec[:, None, None], (N_HEADS, 1, HEAD))
    return dtab, inner, to_end, cdec


def _trunk(x, mod_all, states, pos0, wts, nb, nb_last, passes):
    B, L, _ = x.shape
    depth = mod_all.shape[0]
    c_real = CHUNK if L % CHUNK == 0 else L
    assert L % c_real == 0 and c_real <= CHUNK
    C = -(-c_real // SUBLANES) * SUBLANES
    Lp = L if C == c_real else C
    assert Lp == L or L == c_real
    per_token = Lp < 256
    fuse_conv = states is None and not per_token
    if Lp != L:
        x = jnp.pad(x, ((0, 0), (0, Lp - L), (0, 0)))
    T = B * Lp
    x2 = x.reshape(T, D_MODEL)
    cosf, sinf = _rope_tables(pos0, L, Lp)
    dtab, inner, toend, cdec = _retention_tables(c_real, C)

    tails, layer_states = [], []
    for l in range(depth):
        last = l == depth - 1
        if per_token:
            mod = jnp.repeat(mod_all[l], Lp, axis=0)
        else:
            mod = mod_all[l].reshape(B, 1, 6 * D_MODEL)
        proj, gates, ab, *raw_tail = _in_call(x2, mod, per_token, Lp, wts['norm_mix'][l], wts['w_main'][l],
                                              wts['w_ab'][l], wts['conv_w'][l], fuse_conv)
        proj3 = proj.reshape(B, Lp, MIXER_COLS)
        ob, sg, sr, sh = _mixer_call(
            proj3, ab.reshape(B, Lp, AB_COLS), states, layer_states if last else [],
            wts['conv_w'][l], wts['head_params'][l], wts['gdn_norm'][l], wts['hgrn_norm'][l], wts['hgrn_lb'],
            cosf, sinf, dtab, inner, toend, cdec, C=C, Lv=c_real, nb=nb_last if last else nb, layer=l,
            passes=passes, pre_conv=fuse_conv)
        if fuse_conv:
            tail = raw_tail[0].reshape(B, -1, SUBLANES, CONV_DIM)[:, -1, SUBLANES - (CONV_W - 1):]
        else:
            tail = jnp.concatenate([proj3[:, L - min(L, CONV_W - 1):L, COL[name]:COL[name] + MIX]
                                    for name in CONV_GROUPS], axis=-1)
            if L < CONV_W - 1:
                tail = jnp.concatenate([states[1][l][:, L:], tail], axis=1)
        tails.append(tail)
        layer_states.append((sg, sr, sh))
        x2 = _post_ffn_call(ob.reshape(T, N_BRANCH * MIX), gates, x2, mod, per_token, Lp, wts['norm_ffn'][l],
                            wts['final_norm'], wts['w_branch'][l], wts['w_out'][l], wts['w_gate_up'][l],
                            wts['w_down'][l], final=last)
    y = x2.reshape(B, Lp, D_MODEL)[:, :L]
    sg, sr, sh = layer_states[-1]
    if depth == 1:
        sg, sr, sh = sg[None], sr[None], sh[None]
    return y, (sg, jnp.stack(tails), sr, sh)


def _prep_weights(w_in, conv_w, gdn_a_log, gdn_dt_bias, gdn_norm, hgrn_lb, hgrn_norm, w_branch, w_out,
                  norm_mix, norm_ffn, w_gate_up, w_down, final_norm):
    tiles = []
    for t in range(IN_TILES):
        if t < MIXER_TILES:
            srcs = [MIXER_LAYOUT[3 * t + g][1] for g in range(3)]
            tiles.append(jnp.concatenate([w_in[:, :, c0:c0 + MIX] for c0 in srcs], axis=-1))
        else:
            c0 = MERGE_SRC + (t - MIXER_TILES) * IN_TN
            tiles.append(w_in[:, :, c0:c0 + IN_TN])
    w_main = jnp.stack(tiles, axis=1).astype(BF16)
    w_ab = jnp.pad(w_in[:, :, AB_SRC:AB_SRC + 2 * N_HEADS],
                   ((0, 0), (0, 0), (0, AB_COLS - 2 * N_HEADS))).astype(BF16)
    head_params = jnp.pad(jnp.stack([gdn_a_log, gdn_dt_bias], axis=1),
                          ((0, 0), (0, 0), (0, AB_COLS - N_HEADS))).astype(F32)
    return dict(
        w_main=w_main, w_ab=w_ab, conv_w=conv_w, head_params=head_params, gdn_norm=gdn_norm,
        hgrn_norm=hgrn_norm, hgrn_lb=hgrn_lb, w_branch=w_branch.astype(BF16), w_out=w_out.astype(BF16),
        norm_mix=norm_mix, norm_ffn=norm_ffn, w_gate_up=w_gate_up.astype(BF16),
        w_down=w_down.astype(BF16), final_norm=final_norm)


def kernel(x_prompt, x_sample, state_gdn, state_gdn_conv, state_ret, state_hgrn, c_prompt, c_sample,
           w_in, conv_w, gdn_a_log, gdn_dt_bias, gdn_norm, hgrn_lb, hgrn_norm, w_branch, w_out,
           w_ada, b_ada, norm_mix, norm_ffn, w_gate_up, w_down, final_norm):
    Bp = x_prompt.shape[0]
    wts = _prep_weights(w_in, conv_w, gdn_a_log, gdn_dt_bias, gdn_norm, hgrn_lb, hgrn_norm, w_branch, w_out,
                        norm_mix, norm_ffn, w_gate_up, w_down, final_norm)
    c_all = jnp.concatenate([c_prompt, c_sample], axis=0)
    mod_all = _ada_call(c_all, w_ada, b_ada)
    y_p, st_p = _trunk(x_prompt, mod_all[:, :Bp], None, 0, wts, nb=2, nb_last=2, passes=1)
    y_s, st_s = _trunk(x_sample, mod_all[:, Bp:], (state_gdn, state_gdn_conv, state_ret, state_hgrn),
                       PAST_LEN, wts, nb=8, nb_last=4, passes=1)
    return (y_p, y_s) + st_p + st_s
```

```python
import functools

import jax
import jax.numpy as jnp
from jax import lax
from jax.experimental import pallas as pl
from jax.experimental.pallas import tpu as pltpu

F32 = jnp.float32
BF16 = jnp.bfloat16

D_MODEL = 1024
N_HEADS = 4
HEAD = 128
MIX = N_HEADS * HEAD
CONV_W = 4
CONV_DIM = 3 * MIX
CHUNK = 64
N_BRANCH = 3
D_FF = 2816
ROPE_BASE = 10000.0
EPS = 1e-6
F_TINY = 1e-30
PAST_LEN = 16384

IN_TN = 3 * MIX
MIXER_LAYOUT = (
    ('gdn_q', 0, 'conv'), ('gdn_z', 3 * MIX, 'silu'), ('ret_q', 4 * MIX + 8, None),
    ('gdn_k', MIX, 'conv'), ('ret_k', 5 * MIX + 8, None), ('ret_v', 6 * MIX + 8, None),
    ('gdn_v', 2 * MIX, 'conv'), ('ret_g', 7 * MIX + 8, 'silu'), ('hg_q', 8 * MIX + 8, None),
    ('hg_f', 9 * MIX + 8, None), ('hg_i', 10 * MIX + 8, None), ('hg_g', 11 * MIX + 8, 'sigmoid'),
)
COL = {name: idx * MIX for idx, (name, _, _) in enumerate(MIXER_LAYOUT)}
MIXER_COLS = len(MIXER_LAYOUT) * MIX
MERGE_SRC = 12 * MIX + 8
MAIN_COLS = MIXER_COLS + N_BRANCH * D_MODEL
AB_COLS = 128
AB_SRC = 4 * MIX
IN_TILES = MAIN_COLS // IN_TN
MIXER_TILES = MIXER_COLS // IN_TN
CONV_GROUPS = ('gdn_q', 'gdn_k', 'gdn_v')

SUBLANES = 8
STAGGER_ROUNDS = 4
VMEM_LIMIT = 56 * 1024 * 1024

_DN = {
    'nn': (((1,), (0,)), ((), ())),
    'nt': (((1,), (1,)), ((), ())),
    'tn': (((0,), (0,)), ((), ())),
}


def _split2(a):
    hi = a.astype(BF16)
    lo = (a - hi.astype(F32)).astype(BF16)
    return hi, lo


def _dot(a, b, dims='nn', passes=1):
    dn = _DN[dims]
    if passes == 1:
        return lax.dot_general(a.astype(BF16), b.astype(BF16), dn, preferred_element_type=F32)
    a_hi, a_lo = _split2(a)
    b_hi, b_lo = _split2(b)
    out = lax.dot_general(a_hi, b_lo, dn, preferred_element_type=F32)
    out = out + lax.dot_general(a_lo, b_hi, dn, preferred_element_type=F32)
    return out + lax.dot_general(a_hi, b_hi, dn, preferred_element_type=F32)


def _cumsum_rows(ltri_bf16, x):
    x1 = x.astype(BF16)
    r1 = x - x1.astype(F32)
    x2 = r1.astype(BF16)
    x3 = (r1 - x2.astype(F32)).astype(BF16)
    dn = _DN['nn']
    out = lax.dot_general(ltri_bf16, x3, dn, preferred_element_type=F32)
    out = out + lax.dot_general(ltri_bf16, x2, dn, preferred_element_type=F32)
    return out + lax.dot_general(ltri_bf16, x1, dn, preferred_element_type=F32)


def _silu(x):
    return x * jax.nn.sigmoid(x)


def _rms(x):
    return x * lax.rsqrt(jnp.mean(x * x, axis=-1, keepdims=True) + EPS)


def _ada_kernel(c_ref, w_ref, b_ref, o_ref):
    cs = _silu(c_ref[...])
    o_ref[...] = _dot(cs, w_ref[...]) + b_ref[...]


def _ada_call(c_all, w_ada, b_ada):
    depth = w_ada.shape[0]
    rows = c_all.shape[0]
    n_out = w_ada.shape[2]
    tn = 1536
    return pl.pallas_call(
        _ada_kernel,
        grid=(depth, n_out // tn),
        in_specs=[
            pl.BlockSpec((rows, D_MODEL), lambda l, j: (0, 0)),
            pl.BlockSpec((None, D_MODEL, tn), lambda l, j: (l, 0, j)),
            pl.BlockSpec((None, 1, tn), lambda l, j: (l, 0, j)),
        ],
        out_specs=pl.BlockSpec((None, rows, tn), lambda l, j: (l, 0, j)),
        out_shape=jax.ShapeDtypeStruct((depth, rows, n_out), F32),
        compiler_params=pltpu.CompilerParams(
            dimension_semantics=("arbitrary", "arbitrary"), vmem_limit_bytes=VMEM_LIMIT),
        name="ada",
    )(c_all, w_ada, b_ada.reshape(depth, 1, n_out))


def _row_tile(n_rows, preferred):
    tm = preferred
    while n_rows % tm:
        tm //= 2
    return tm


def _mod_spec(per_token, tm, tiles_per_seq, seg, ngrid):
    if per_token:
        if ngrid == 2:
            return pl.BlockSpec((tm, D_MODEL), lambda i, j: (i, seg))
        return pl.BlockSpec((tm, D_MODEL), lambda i: (i, seg))
    if ngrid == 2:
        return pl.BlockSpec((None, 1, D_MODEL), lambda i, j: (i // tiles_per_seq, 0, seg))
    return pl.BlockSpec((None, 1, D_MODEL), lambda i: (i // tiles_per_seq, 0, seg))


def _in_kernel(*refs, fuse_conv, tiles_per_seq):
    if fuse_conv:
        (x_ref, sh_ref, sc_ref, nw_ref, w_ref, wab_ref, cw_ref,
         o_ref, g_ref, ab_ref, tail_ref, h_scr, xp_scr, carry_scr) = refs
    else:
        x_ref, sh_ref, sc_ref, nw_ref, w_ref, wab_ref, o_ref, g_ref, ab_ref, h_scr = refs
    i = pl.program_id(0)
    j = pl.program_id(1)
    tm = x_ref.shape[0]

    def conv_group(c, raw):
        lo = c * MIX
        xp_scr[0:SUBLANES, :] = carry_scr[:, lo:lo + MIX]
        xp_scr[SUBLANES:SUBLANES + tm, :] = raw
        conv = xp_scr[SUBLANES:SUBLANES + tm, :] * cw_ref[CONV_W - 1:CONV_W, lo:lo + MIX]
        for t in range(CONV_W - 1):
            off = SUBLANES - (CONV_W - 1) + t
            conv = conv + xp_scr[off:off + tm, :] * cw_ref[t:t + 1, lo:lo + MIX]
        carry_scr[:, lo:lo + MIX] = xp_scr[tm:tm + SUBLANES, :]
        tail_ref[:, lo:lo + MIX] = xp_scr[tm:tm + SUBLANES, :]
        act = _silu(conv)
        if c == 2:
            o_ref[:, 0:MIX] = act
            return
        for hh in range(N_HEADS):
            a = act[:, hh * HEAD:(hh + 1) * HEAD]
            a = a * lax.rsqrt(jnp.sum(a * a, axis=-1, keepdims=True) + EPS)
            if c == 0:
                a = a * (HEAD ** -0.5)
            o_ref[:, hh * HEAD:(hh + 1) * HEAD] = a

    def tile(t, hb):
        for g in range(3):
            epilogue = MIXER_LAYOUT[3 * t + g][2]
            out = jnp.dot(hb, w_ref[t, :, g * MIX:(g + 1) * MIX], preferred_element_type=F32)
            if epilogue == 'conv' and fuse_conv:
                conv_group(t, out)
                continue
            if epilogue == 'silu':
                out = _silu(out)
            elif epilogue == 'sigmoid':
                out = jax.nn.sigmoid(out)
            o_ref[:, g * MIX:(g + 1) * MIX] = out

    @pl.when(j == 0)
    def _():
        h = _rms(x_ref[...]) * nw_ref[...]
        h = h * (1.0 + sc_ref[...]) + sh_ref[...]
        hb = h.astype(BF16)
        h_scr[...] = hb
        ab_ref[...] = jnp.dot(hb, wab_ref[...], preferred_element_type=F32)
        if fuse_conv:
            @pl.when(i % tiles_per_seq == 0)
            def _():
                carry_scr[...] = jnp.zeros_like(carry_scr)
        tile(0, hb)

    for t in range(1, MIXER_TILES):
        @pl.when(j == t)
        def _(t=t):
            tile(t, h_scr[...])

    @pl.when(j >= MIXER_TILES)
    def _():
        acc = jnp.dot(h_scr[...], w_ref[j], preferred_element_type=F32)
        g_ref[...] = jax.nn.sigmoid(acc).astype(BF16)


def _in_call(x2, mod, per_token, seq_len, norm_w, w_main, w_ab, conv_w, fuse_conv):
    T = x2.shape[0]
    tm = _row_tile(T if per_token else seq_len, 512 if per_token else 1024)
    tn = IN_TN
    tps = max(seq_len // tm, 1)
    last_gate = IN_TILES - MIXER_TILES - 1
    assert not fuse_conv or not per_token
    assert all(MIXER_LAYOUT[3 * c][0] == name for c, name in enumerate(CONV_GROUPS))

    def gate_block(i, j):
        on_gate = j >= MIXER_TILES
        row = jnp.where(on_gate, i, jnp.maximum(i - 1, 0))
        col = jnp.where(on_gate, j - MIXER_TILES, jnp.where(i > 0, last_gate, 0))
        return row, col

    in_specs = [
        pl.BlockSpec((tm, D_MODEL), lambda i, j: (i, 0)),
        _mod_spec(per_token, tm, tps, 0, 2),
        _mod_spec(per_token, tm, tps, 1, 2),
        pl.BlockSpec((1, D_MODEL), lambda i, j: (0, 0)),
        pl.BlockSpec((IN_TILES, D_MODEL, tn), lambda i, j: (0, 0, 0), pipeline_mode=pl.Buffered(1)),
        pl.BlockSpec((D_MODEL, AB_COLS), lambda i, j: (0, 0)),
    ]
    out_specs = [
        pl.BlockSpec((tm, tn), lambda i, j: (i, jnp.minimum(j, MIXER_TILES - 1))),
        pl.BlockSpec((tm, tn), gate_block),
        pl.BlockSpec((tm, AB_COLS), lambda i, j: (i, 0)),
    ]
    out_shape = [
        jax.ShapeDtypeStruct((T, MIXER_COLS), F32),
        jax.ShapeDtypeStruct((T, N_BRANCH * D_MODEL), BF16),
        jax.ShapeDtypeStruct((T, AB_COLS), F32),
    ]
    scratch = [pltpu.VMEM((tm, D_MODEL), BF16)]
    operands = [x2, mod, mod, norm_w.reshape(1, D_MODEL), w_main, w_ab]
    if fuse_conv:
        in_specs.append(pl.BlockSpec((CONV_W, CONV_DIM), lambda i, j: (0, 0)))
        operands.append(conv_w)
        out_specs.append(pl.BlockSpec((SUBLANES, CONV_DIM), lambda i, j: (i, 0)))
        out_shape.append(jax.ShapeDtypeStruct((T // tm * SUBLANES, CONV_DIM), F32))
        scratch += [pltpu.VMEM((tm + SUBLANES, MIX), F32), pltpu.VMEM((SUBLANES, CONV_DIM), F32)]
    return pl.pallas_call(
        functools.partial(_in_kernel, fuse_conv=fuse_conv, tiles_per_seq=tps),
        grid=(T // tm, IN_TILES),
        in_specs=in_specs,
        out_specs=out_specs,
        out_shape=out_shape,
        scratch_shapes=scratch,
        compiler_params=pltpu.CompilerParams(
            dimension_semantics=("arbitrary", "arbitrary"), vmem_limit_bytes=VMEM_LIMIT),
        name="in_proj",
    )(*operands)


def _mixer_kernel(*refs, C, Lv, nb, layer, passes, has_state, depth, pre_conv):
    proj_ref, ab_ref = refs[:2]
    pos = 2
    if has_state:
        conv0_ref, sg0_ref, sr0_ref, sh0_ref = refs[pos:pos + 4]
        pos += 4
    (cw_ref, hp_ref, gn_ref, hn_ref, lb_ref, cos_ref, sin_ref,
     dtab_ref, inner_ref, toend_ref, cdec_ref) = refs[pos:pos + 11]
    pos += 11 + (3 if layer > 0 else 0)
    ob_ref, sg_ref, sr_ref, sh_ref, hg_ref = refs[pos:pos + 5]
    xpad_ref = None if pre_conv else refs[-1]
    stacked = (sg_ref, sr_ref, sh_ref)
    if layer == 0:
        sg_ref, sr_ref, sh_ref = (r.at[0] for r in stacked)
    tail0 = SUBLANES - (CONV_W - 1)

    @pl.when(pl.program_id(1) == 0)
    def _():
        if layer == 0 and depth > 1:
            for r in stacked:
                r[1:] = jnp.zeros((depth - 1,) + r.shape[1:], F32)
        if has_state:
            sg_ref[...] = sg0_ref[...]
            sr_ref[...] = sr0_ref[...]
            sh_ref[...] = sh0_ref[...]
            xpad_ref[:, tail0:SUBLANES, :] = conv0_ref[...]
        else:
            sg_ref[...] = jnp.zeros_like(sg_ref)
            sr_ref[...] = jnp.zeros_like(sr_ref)
            sh_ref[...] = jnp.zeros_like(sh_ref)
            if not pre_conv:
                xpad_ref[:, tail0:SUBLANES, :] = jnp.zeros((nb, CONV_W - 1, CONV_DIM), F32)

    padded = Lv < C
    row1 = lax.broadcasted_iota(jnp.int32, (C, 1), 0)
    valid = row1 < Lv
    ri = lax.broadcasted_iota(jnp.int32, (C, C), 0)
    ci = lax.broadcasted_iota(jnp.int32, (C, C), 1)
    incl = ri >= ci
    strict = ri > ci
    eye = ri == ci
    ltri = jnp.where(incl, 1.0, 0.0).astype(BF16)
    r128 = lax.broadcasted_iota(jnp.int32, (HEAD, HEAD), 0)
    c128 = lax.broadcasted_iota(jnp.int32, (HEAD, HEAD), 1)
    eye128 = r128 == c128
    n_levels = max((Lv - 1).bit_length(), 1)

    lbp = lb_ref[...]
    lbe = jnp.exp(lbp - jnp.max(lbp, axis=0, keepdims=True))
    lbw = lbe / jnp.sum(lbe, axis=0, keepdims=True)
    lb_row = lbw[0:1, :]
    for d in range(1, layer + 1):
        lb_row = lb_row + lbw[d:d + 1, :]
    lb_row = lb_row - lbw[0:1, :]

    neg_a = -jnp.exp(hp_ref[0:1, :])
    dt_b = hp_ref[1:2, :]
    gn = gn_ref[...]
    hn = hn_ref[...]
    cosf = cos_ref[...]
    sinf = sin_ref[...]

    SB = SUBLANES

    def head_cols(n, name, h):
        return proj_ref[n, :, COL[name] + h * HEAD:COL[name] + (h + 1) * HEAD]

    def gdn_task(n, h, qkv, G_all, beta_all):
        if pre_conv:
            q = head_cols(n, 'gdn_q', h)
            k = head_cols(n, 'gdn_k', h)
            v = head_cols(n, 'gdn_v', h)
        else:
            q = qkv[:, h * HEAD:(h + 1) * HEAD]
            k = qkv[:, MIX + h * HEAD:MIX + (h + 1) * HEAD]
            v = qkv[:, 2 * MIX + h * HEAD:2 * MIX + (h + 1) * HEAD]
            q = q * lax.rsqrt(jnp.sum(q * q, axis=-1, keepdims=True) + EPS) * (HEAD ** -0.5)
            k = k * lax.rsqrt(jnp.sum(k * k, axis=-1, keepdims=True) + EPS)
        kk = _dot(k, k, 'nt', passes)
        qk = _dot(q, k, 'nt', passes)
        yield
        Gc = G_all[:, h:h + 1]
        beta = beta_all[:, N_HEADS + h:N_HEADS + h + 1]
        Gr = jnp.sum(jnp.where(eye, Gc, 0.0), axis=0, keepdims=True)
        G_last = G_all[C - 1:C, h:h + 1]
        eG = jnp.exp(Gc)
        decay = jnp.where(incl, jnp.exp(jnp.where(incl, Gc - Gr, 0.0)), 0.0)
        P = -jnp.where(strict, beta * kk * decay, 0.0)
        sol = jnp.concatenate([beta * v, (beta * eG) * k], axis=-1)
        for lvl in range(n_levels):
            upd = _dot(P, sol, 'nn', passes)
            if lvl + 1 < n_levels:
                P = _dot(P, P, 'nn', passes)
            yield
            sol = sol + upd
        S = sg_ref[n, h]
        sk_s = _dot(sol[:, HEAD:], S, 'nn', passes)
        q_s = _dot(q, S, 'nn', passes)
        yield
        u = sol[:, :HEAD] - sk_s
        k_end = k * jnp.exp(G_last - Gc)
        intra = _dot(qk * decay, u, 'nn', passes)
        s_upd = _dot(k_end, u, 'tn', passes)
        yield
        o = eG * q_s + intra
        sg_ref[n, h] = jnp.exp(G_last) * S + s_upd
        z_act = head_cols(n, 'gdn_z', h)
        ob_ref[n, :, h * HEAD:(h + 1) * HEAD] = ((_rms(o) * gn) * z_act).astype(ob_ref.dtype)

    def ret_task(n, h):
        q = head_cols(n, 'ret_q', h)
        k = head_cols(n, 'ret_k', h)
        v = head_cols(n, 'ret_v', h)
        q = q * cosf + pltpu.roll(q, HEAD // 2, 1) * sinf
        k = (k * cosf + pltpu.roll(k, HEAD // 2, 1) * sinf) * (HEAD ** -0.5)
        if padded:
            v = jnp.where(valid, v, 0.0)
        S = sr_ref[n, h]
        qk = _dot(q, k, 'nt', passes)
        q_s = _dot(q, S, 'nn', passes)
        s_upd = _dot(k * toend_ref[h], v, 'tn', passes)
        yield
        intra = _dot(qk * dtab_ref[h], v, 'nn', passes)
        sr_ref[n, h] = cdec_ref[h] * S + s_upd
        yield
        o = intra + inner_ref[h] * q_s
        g_act = head_cols(n, 'ret_g', h)
        mu = jnp.mean(o, axis=-1, keepdims=True)
        oc = o - mu
        var = jnp.mean(oc * oc, axis=-1, keepdims=True)
        ob_ref[n, :, MIX + h * HEAD:MIX + (h + 1) * HEAD] = (
            oc * lax.rsqrt(var + EPS) * g_act).astype(ob_ref.dtype)

    def hg_task(n, h):
        qc = head_cols(n, 'hg_q', h)
        zf = head_cols(n, 'hg_f', h)
        vc = head_cols(n, 'hg_i', h)
        lb_h = lb_row[:, h * HEAD:(h + 1) * HEAD]
        s_neg = jax.nn.sigmoid(-zf)
        f_gate = jax.nn.sigmoid(zf) + lb_h * s_neg
        log_f = jnp.log(jnp.maximum(f_gate, F_TINY))
        kc = (1.0 - lb_h) * s_neg
        if padded:
            log_f = jnp.where(valid, log_f, 0.0)
            kc = jnp.where(valid, kc, 0.0)
            vc = jnp.where(valid, vc, 0.0)
        A = _cumsum_rows(ltri, log_f)
        yield
        hgs = hg_ref.at[n, h]
        hgs[0] = A
        hgs[1] = kc
        S = sh_ref[n, h]
        a_last = hgs[0, C - 1:C, :]
        q_s = _dot(qc * jnp.exp(A), S, 'nn', passes)
        s_upd = _dot(kc * jnp.exp(a_last - A), vc, 'tn', passes)
        score_rows = []
        for blk in range(C // SB):
            r0 = blk * SB
            qI = qc[r0:r0 + SB]
            AI = A[r0:r0 + SB]
            rowg = r0 + lax.broadcasted_iota(jnp.int32, (SB, C), 0)
            colg = lax.broadcasted_iota(jnp.int32, (SB, C), 1)
            cross = None
            if r0 > 0:
                a_b = hgs[0, r0 - 1:r0, :]
                kt = jnp.concatenate([kc[:r0] * jnp.exp(a_b - A[:r0]), jnp.zeros((C - r0, HEAD), F32)], axis=0)
                qt = qI * jnp.exp(AI - a_b)
                cross = _dot(qt, kt, 'nt', passes)
            sc_blk = jnp.zeros((SB, C), F32)
            for s in range(SB):
                a_s = hgs[0, r0 + s:r0 + s + 1, :]
                k_s = hgs[1, r0 + s:r0 + s + 1, :]
                w = jnp.exp(AI - a_s)
                val = jnp.sum(qI * k_s * w, axis=-1, keepdims=True)
                sc_blk = jnp.where(colg == r0 + s, val, sc_blk)
            yield
            if cross is not None:
                sc_blk = sc_blk + cross
            score_rows.append(jnp.where(rowg >= colg, sc_blk, 0.0))
        scores = score_rows[0] if len(score_rows) == 1 else jnp.concatenate(score_rows, axis=0)
        intra = _dot(scores, vc, 'nn', passes)
        dec_col = jnp.sum(jnp.where(eye128, jnp.exp(a_last), 0.0), axis=1, keepdims=True)
        sh_ref[n, h] = dec_col * S + s_upd
        yield
        o = intra + q_s
        g_act = head_cols(n, 'hg_g', h)
        ob_ref[n, :, 2 * MIX + h * HEAD:2 * MIX + (h + 1) * HEAD] = (
            (_rms(o) * hn) * g_act).astype(ob_ref.dtype)

    stagger = STAGGER_ROUNDS if nb <= 2 else 0
    gdn_tasks, other_tasks = [], []
    for n in range(nb):
        qkv = None
        if not pre_conv:
            for c, name in enumerate(CONV_GROUPS):
                xpad_ref[n, SUBLANES:SUBLANES + C, c * MIX:(c + 1) * MIX] = proj_ref[n, :, COL[name]:COL[name] + MIX]
            conv = xpad_ref[n, SUBLANES:SUBLANES + C, :] * cw_ref[CONV_W - 1:CONV_W, :]
            for j in range(CONV_W - 1):
                off = SUBLANES - (CONV_W - 1) + j
                conv = conv + xpad_ref[n, off:off + C, :] * cw_ref[j:j + 1, :]
            xpad_ref[n, 0:SUBLANES, :] = xpad_ref[n, C:C + SUBLANES, :]
            qkv = _silu(conv)
        ab = ab_ref[n]
        g_all = neg_a * jax.nn.softplus(ab + dt_b)
        beta_all = jax.nn.sigmoid(ab)
        if padded:
            g_all = jnp.where(valid, g_all, 0.0)
            beta_all = jnp.where(valid, beta_all, 0.0)
        G_all = _cumsum_rows(ltri, g_all)
        for h in range(N_HEADS):
            gdn_tasks.append([n * stagger, gdn_task(n, h, qkv, G_all, beta_all)])
            other_tasks.append([n * stagger, hg_task(n, h)])
            other_tasks.append([n * stagger, ret_task(n, h)])
    tasks = gdn_tasks + other_tasks
    while tasks:
        alive = []
        for t in tasks:
            if t[0] > 0:
                t[0] -= 1
                alive.append(t)
                continue
            try:
                next(t[1])
                alive.append(t)
            except StopIteration:
                pass
        tasks = alive


def _mixer_call(proj3, ab3, states, stacked_so_far, conv_w, head_params, gdn_norm, hg_norm, hgrn_lb,
                cosf, sinf, dtab, inner, toend, cdec, *, C, Lv, nb, layer, passes, pre_conv):
    B, Lp, _ = proj3.shape
    n_chunks = Lp // C
    depth = hgrn_lb.shape[0]
    in_state_spec = pl.BlockSpec((None, nb, N_HEADS, HEAD, HEAD), lambda b, i: (layer, b, 0, 0, 0))
    if layer == 0:
        out_state_spec = pl.BlockSpec((depth, nb, N_HEADS, HEAD, HEAD), lambda b, i: (0, b, 0, 0, 0))
    else:
        out_state_spec = in_state_spec
    state_shape = jax.ShapeDtypeStruct((depth, B, N_HEADS, HEAD, HEAD), F32)
    ob_dtype = BF16 if C % (2 * SUBLANES) == 0 else F32

    def whole(shape):
        return pl.BlockSpec(shape, lambda b, i: (0,) * len(shape))

    has_state = states is not None
    operands = [proj3, ab3]
    in_specs = [
        pl.BlockSpec((nb, C, MIXER_COLS), lambda b, i: (b, i, 0)),
        pl.BlockSpec((nb, C, AB_COLS), lambda b, i: (b, i, 0)),
    ]
    if has_state:
        operands += [states[1], states[0], states[2], states[3]]
        in_specs += [pl.BlockSpec((None, nb, CONV_W - 1, CONV_DIM), lambda b, i: (layer, b, 0, 0)),
                     in_state_spec, in_state_spec, in_state_spec]
    operands += [conv_w, head_params, gdn_norm.reshape(1, HEAD), hg_norm.reshape(1, HEAD), hgrn_lb,
                 cosf, sinf, dtab, inner, toend, cdec]

    aliased = list(stacked_so_far) if layer > 0 else []
    aliases = {len(operands) + k: 1 + k for k in range(len(aliased))}
    kern = functools.partial(_mixer_kernel, C=C, Lv=Lv, nb=nb, layer=layer, passes=passes,
                             has_state=has_state, depth=depth, pre_conv=pre_conv)
    scratch = [pltpu.VMEM((nb, N_HEADS, 2, C, HEAD), F32)]
    if not pre_conv:
        scratch.append(pltpu.VMEM((nb, C + SUBLANES, CONV_DIM), F32))
    return pl.pallas_call(
        kern,
        grid=(B // nb, n_chunks),
        in_specs=in_specs + [
            whole((CONV_W, CONV_DIM)),
            whole((2, AB_COLS)),
            whole((1, HEAD)),
            whole((1, HEAD)),
            whole((depth, MIX)),
            pl.BlockSpec((C, HEAD), lambda b, i: (i, 0)),
            pl.BlockSpec((C, HEAD), lambda b, i: (i, 0)),
            whole((N_HEADS, C, C)),
            whole((N_HEADS, C, HEAD)),
            whole((N_HEADS, C, HEAD)),
            whole((N_HEADS, 1, HEAD)),
        ] + [pl.BlockSpec(memory_space=pl.ANY)] * len(aliased),
        out_specs=[
            pl.BlockSpec((nb, C, N_BRANCH * MIX), lambda b, i: (b, i, 0)),
            out_state_spec, out_state_spec, out_state_spec,
        ],
        out_shape=[
            jax.ShapeDtypeStruct((B, Lp, N_BRANCH * MIX), ob_dtype),
            state_shape, state_shape, state_shape,
        ],
        input_output_aliases=aliases,
        scratch_shapes=scratch,
        compiler_params=pltpu.CompilerParams(
            dimension_semantics=("arbitrary", "arbitrary"), vmem_limit_bytes=VMEM_LIMIT),
        name="mixer",
    )(*operands, *aliased)


FF_SPLITS = (0, 1024, 2048, D_FF)


def _post_ffn_kernel(ob_ref, mg_ref, x_ref, gtm_ref, sh_ref, sc_ref, gtf_ref, nw_ref, fw_ref,
                     wb_ref, wo_ref, wg_ref, wu_ref, wd_ref, o_ref, *, final):
    merged = None
    for n in range(N_BRANCH):
        br = _dot(ob_ref[:, n * MIX:(n + 1) * MIX], wb_ref[n])
        term = mg_ref[:, n * D_MODEL:(n + 1) * D_MODEL].astype(F32) * br
        merged = term if merged is None else merged + term
    x = x_ref[...] + gtm_ref[...] * _dot(merged, wo_ref[...])
    h = (_rms(x) * nw_ref[...] * (1.0 + sc_ref[...]) + sh_ref[...]).astype(BF16)
    y = None
    for lo, hi in zip(FF_SPLITS[:-1], FF_SPLITS[1:]):
        g = jnp.dot(h, wg_ref[:, lo:hi], preferred_element_type=F32)
        u = jnp.dot(h, wu_ref[:, lo:hi], preferred_element_type=F32)
        part = _dot(_silu(g) * u, wd_ref[lo:hi, :])
        y = part if y is None else y + part
    xn = x + gtf_ref[...] * y
    if final:
        xn = _rms(xn) * fw_ref[...]
    o_ref[...] = xn


def _post_ffn_call(ob2, gates, x2, mod, per_token, seq_len, norm_w, final_w, w_branch, w_out, w_gate_up, w_down,
                   final):
    T = x2.shape[0]
    tm = _row_tile(T if per_token else seq_len, 256 if per_token else 512)
    tps = max(seq_len // tm, 1)
    resident = pl.Buffered(1)
    return pl.pallas_call(
        functools.partial(_post_ffn_kernel, final=final),
        grid=(T // tm,),
        in_specs=[
            pl.BlockSpec((tm, N_BRANCH * MIX), lambda i: (i, 0)),
            pl.BlockSpec((tm, N_BRANCH * D_MODEL), lambda i: (i, 0)),
            pl.BlockSpec((tm, D_MODEL), lambda i: (i, 0)),
            _mod_spec(per_token, tm, tps, 2, 1),
            _mod_spec(per_token, tm, tps, 3, 1),
            _mod_spec(per_token, tm, tps, 4, 1),
            _mod_spec(per_token, tm, tps, 5, 1),
            pl.BlockSpec((1, D_MODEL), lambda i: (0, 0)),
            pl.BlockSpec((1, D_MODEL), lambda i: (0, 0)),
            pl.BlockSpec((N_BRANCH, MIX, D_MODEL), lambda i: (0, 0, 0), pipeline_mode=resident),
            pl.BlockSpec((D_MODEL, D_MODEL), lambda i: (0, 0), pipeline_mode=resident),
            pl.BlockSpec((D_MODEL, D_FF), lambda i: (0, 0), pipeline_mode=resident),
            pl.BlockSpec((D_MODEL, D_FF), lambda i: (0, 1), pipeline_mode=resident),
            pl.BlockSpec((D_FF, D_MODEL), lambda i: (0, 0), pipeline_mode=resident),
        ],
        out_specs=pl.BlockSpec((tm, D_MODEL), lambda i: (i, 0)),
        out_shape=jax.ShapeDtypeStruct((T, D_MODEL), F32),
        compiler_params=pltpu.CompilerParams(
            dimension_semantics=("arbitrary",), vmem_limit_bytes=VMEM_LIMIT),
        name="post_ffn",
    )(ob2, gates, x2, mod, mod, mod, mod, norm_w.reshape(1, D_MODEL), final_w.reshape(1, D_MODEL),
      w_branch, w_out, w_gate_up, w_gate_up, w_down)


def _rope_tables(pos0, length, padded_len):
    half = HEAD // 2
    pos = pos0 + jnp.arange(padded_len, dtype=jnp.int32)
    inv = ROPE_BASE ** (-jnp.arange(half, dtype=F32) / half)
    ang = pos.astype(F32)[:, None] * inv[None, :]
    cos, sin = jnp.cos(ang), jnp.sin(ang)
    return jnp.concatenate([cos, cos], axis=-1), jnp.concatenate([-sin, sin], axis=-1)


def _retention_tables(c_real, c_pad):
    log_gamma = jnp.log1p(-jnp.exp2(-5.0 - jnp.arange(N_HEADS, dtype=F32)))
    idx = jnp.arange(c_real, dtype=F32)
    rel = idx[:, None] - idx[None, :]
    mask = rel >= 0
    lg = log_gamma[:, None, None]
    dtab = jnp.where(mask, jnp.exp(jnp.where(mask, lg * rel, 0.0)), 0.0)
    inner = jnp.exp(log_gamma[:, None] * (idx + 1.0))
    to_end = jnp.exp(log_gamma[:, None] * (c_real - 1.0 - idx))
    cdec = jnp.exp(log_gamma * c_real)
    p = c_pad - c_real
    dtab = jnp.pad(dtab, ((0, 0), (0, p), (0, p)))
    inner = jnp.broadcast_to(jnp.pad(inner, ((0, 0), (0, p)))[:, :, None], (N_HEADS, c_pad, HEAD))
    to_end = jnp.broadcast_to(jnp.pad(to_end, ((0, 0), (0, p)))[:, :, None], (N_HEADS, c_pad, HEAD))
    cdec = jnp.broadcast_to(cdec[:, None, None], (N_HEADS, 1, HEAD))
    return dtab, inner, to_end, cdec


def _trunk(x, mod_all, states, pos0, wts, nb, passes):
    B, L, _ = x.shape
    depth = mod_all.shape[0]
    c_real = CHUNK if L % CHUNK == 0 else L
    assert L % c_real == 0 and c_real <= CHUNK
    C = -(-c_real // SUBLANES) * SUBLANES
    Lp = L if C == c_real else C
    assert Lp == L or L == c_real
    per_token = Lp < 256
    fuse_conv = states is None and not per_token
    if Lp != L:
        x = jnp.pad(x, ((0, 0), (0, Lp - L), (0, 0)))
    T = B * Lp
    x2 = x.reshape(T, D_MODEL)
    cosf, sinf = _rope_tables(pos0, L, Lp)
    dtab, inner, toend, cdec = _retention_tables(c_real, C)

    tails, stacked = [], None
    for l in range(depth):
        last = l == depth - 1
        if per_token:
            mod = jnp.repeat(mod_all[l], Lp, axis=0)
        else:
            mod = mod_all[l].reshape(B, 1, 6 * D_MODEL)
        proj, gates, ab, *raw_tail = _in_call(x2, mod, per_token, Lp, wts['norm_mix'][l], wts['w_main'][l],
                                              wts['w_ab'][l], wts['conv_w'][l], fuse_conv)
        proj3 = proj.reshape(B, Lp, MIXER_COLS)
        ob, *stacked = _mixer_call(
            proj3, ab.reshape(B, Lp, AB_COLS), states, stacked,
            wts['conv_w'][l], wts['head_params'][l], wts['gdn_norm'][l], wts['hgrn_norm'][l], wts['hgrn_lb'],
            cosf, sinf, dtab, inner, toend, cdec, C=C, Lv=c_real, nb=nb, layer=l,
            passes=passes, pre_conv=fuse_conv)
        if fuse_conv:
            tail = raw_tail[0].reshape(B, -1, SUBLANES, CONV_DIM)[:, -1, SUBLANES - (CONV_W - 1):]
        else:
            tail = jnp.concatenate([proj3[:, L - min(L, CONV_W - 1):L, COL[name]:COL[name] + MIX]
                                    for name in CONV_GROUPS], axis=-1)
            if L < CONV_W - 1:
                tail = jnp.concatenate([states[1][l][:, L:], tail], axis=1)
        tails.append(tail)
        x2 = _post_ffn_call(ob.reshape(T, N_BRANCH * MIX), gates, x2, mod, per_token, Lp, wts['norm_ffn'][l],
                            wts['final_norm'], wts['w_branch'][l], wts['w_out'][l], wts['w_gate_up'][l],
                            wts['w_down'][l], final=last)
    y = x2.reshape(B, Lp, D_MODEL)[:, :L]
    sg, sr, sh = stacked
    return y, (sg, jnp.stack(tails), sr, sh)


LANES = 128
GROUP_SRC = tuple(src for _, src, _ in MIXER_LAYOUT) + tuple(
    MERGE_SRC + k * MIX for k in range(N_BRANCH * D_MODEL // MIX))


def _regroup_kernel(a_tab, sh_tab, a_ref, b_ref, o_ref):
    del a_tab
    shifted = sh_tab[pl.program_id(1)] != 0
    x = a_ref[...]

    @pl.when(shifted)
    def _():
        wide = jnp.concatenate([x, b_ref[...]], axis=-1)
        skip = 2 * N_HEADS
        o_ref[...] = pltpu.roll(wide, MIX + LANES - skip, 1)[:, :MIX].astype(BF16)

    @pl.when(jnp.logical_not(shifted))
    def _():
        o_ref[...] = x.astype(BF16)


def _regroup_w_in(w_in):
    depth = w_in.shape[0]
    n_groups = len(GROUP_SRC)
    a_tab = jnp.asarray([src // MIX for src in GROUP_SRC], jnp.int32)
    sh_tab = jnp.asarray([src % MIX != 0 for src in GROUP_SRC], jnp.int32)
    assert all(src % MIX in (0, 2 * N_HEADS) for src in GROUP_SRC)
    per_tile = IN_TN // MIX
    grid_spec = pltpu.PrefetchScalarGridSpec(
        num_scalar_prefetch=2,
        grid=(depth, n_groups),
        in_specs=[
            pl.BlockSpec((None, D_MODEL, MIX), lambda l, g, a, s: (l, 0, a[g])),
            pl.BlockSpec((None, D_MODEL, LANES), lambda l, g, a, s: (l, 0, (a[g] + 1) * (MIX // LANES))),
        ],
        out_specs=pl.BlockSpec((None, None, D_MODEL, MIX), lambda l, g, a, s: (l, g // per_tile, 0, g % per_tile)),
    )
    return pl.pallas_call(
        _regroup_kernel,
        grid_spec=grid_spec,
        out_shape=jax.ShapeDtypeStruct((depth, IN_TILES, D_MODEL, IN_TN), BF16),
        compiler_params=pltpu.CompilerParams(
            dimension_semantics=("arbitrary", "arbitrary"), vmem_limit_bytes=VMEM_LIMIT),
        name="regroup_w_in",
    )(a_tab, sh_tab, w_in, w_in)


def _prep_weights(w_in, conv_w, gdn_a_log, gdn_dt_bias, gdn_norm, hgrn_lb, hgrn_norm, w_branch, w_out,
                  norm_mix, norm_ffn, w_gate_up, w_down, final_norm):
    w_main = _regroup_w_in(w_in)
    w_ab = jnp.pad(w_in[:, :, AB_SRC:AB_SRC + 2 * N_HEADS],
                   ((0, 0), (0, 0), (0, AB_COLS - 2 * N_HEADS))).astype(BF16)
    head_params = jnp.pad(jnp.stack([gdn_a_log, gdn_dt_bias], axis=1),
                          ((0, 0), (0, 0), (0, AB_COLS - N_HEADS))).astype(F32)
    return dict(
        w_main=w_main, w_ab=w_ab, conv_w=conv_w, head_params=head_params, gdn_norm=gdn_norm,
        hgrn_norm=hgrn_norm, hgrn_lb=hgrn_lb, w_branch=w_branch.astype(BF16), w_out=w_out.astype(BF16),
        norm_mix=norm_mix, norm_ffn=norm_ffn, w_gate_up=w_gate_up.astype(BF16),
        w_down=w_down.astype(BF16), final_norm=final_norm)


def kernel(x_prompt, x_sample, state_gdn, state_gdn_conv, state_ret, state_hgrn, c_prompt, c_sample,
           w_in, conv_w, gdn_a_log, gdn_dt_bias, gdn_norm, hgrn_lb, hgrn_norm, w_branch, w_out,
           w_ada, b_ada, norm_mix, norm_ffn, w_gate_up, w_down, final_norm):
    Bp = x_prompt.shape[0]
    wts = _prep_weights(w_in, conv_w, gdn_a_log, gdn_dt_bias, gdn_norm, hgrn_lb, hgrn_norm, w_branch, w_out,
                        norm_mix, norm_ffn, w_gate_up, w_down, final_norm)
    c_all = jnp.concatenate([c_prompt, c_sample], axis=0)
    mod_all = _ada_call(c_all, w_ada, b_ada)
    y_p, st_p = _trunk(x_prompt, mod_all[:, :Bp], None, 0, wts, nb=2, passes=1)
    y_s, st_s = _trunk(x_sample, mod_all[:, Bp:], (state_gdn, state_gdn_conv, state_ret, state_hgrn),
                       PAST_LEN, wts, nb=8, passes=1)
    return (y_p, y_s) + st_p + st_s
```

```python
import functools

import jax
import jax.numpy as jnp
from jax import lax
from jax.experimental import pallas as pl
from jax.experimental.pallas import tpu as pltpu

F32 = jnp.float32
BF16 = jnp.bfloat16

D_MODEL = 1024
N_HEADS = 4
HEAD = 128
MIX = N_HEADS * HEAD
CONV_W = 4
CONV_DIM = 3 * MIX
CHUNK = 64
N_BRANCH = 3
D_FF = 2816
ROPE_BASE = 10000.0
EPS = 1e-6
F_TINY = 1e-30
PAST_LEN = 16384

IN_TN = 3 * MIX
MIXER_LAYOUT = (
    ('gdn_q', 0, 'conv'), ('gdn_z', 3 * MIX, 'silu'), ('ret_q', 4 * MIX + 8, None),
    ('gdn_k', MIX, 'conv'), ('ret_k', 5 * MIX + 8, None), ('ret_v', 6 * MIX + 8, None),
    ('gdn_v', 2 * MIX, 'conv'), ('ret_g', 7 * MIX + 8, 'silu'), ('hg_q', 8 * MIX + 8, None),
    ('hg_f', 9 * MIX + 8, None), ('hg_i', 10 * MIX + 8, None), ('hg_g', 11 * MIX + 8, 'sigmoid'),
)
COL = {name: idx * MIX for idx, (name, _, _) in enumerate(MIXER_LAYOUT)}
MIXER_COLS = len(MIXER_LAYOUT) * MIX
MERGE_SRC = 12 * MIX + 8
MAIN_COLS = MIXER_COLS + N_BRANCH * D_MODEL
AB_COLS = 128
AB_SRC = 4 * MIX
IN_TILES = MAIN_COLS // IN_TN
MIXER_TILES = MIXER_COLS // IN_TN
CONV_GROUPS = ('gdn_q', 'gdn_k', 'gdn_v')

SUBLANES = 8
STAGGER_ROUNDS = 4
VMEM_LIMIT = 56 * 1024 * 1024

_DN = {
    'nn': (((1,), (0,)), ((), ())),
    'nt': (((1,), (1,)), ((), ())),
    'tn': (((0,), (0,)), ((), ())),
}


def _split2(a):
    hi = a.astype(BF16)
    lo = (a - hi.astype(F32)).astype(BF16)
    return hi, lo


def _dot(a, b, dims='nn', passes=1):
    dn = _DN[dims]
    if passes == 1:
        return lax.dot_general(a.astype(BF16), b.astype(BF16), dn, preferred_element_type=F32)
    a_hi, a_lo = _split2(a)
    b_hi, b_lo = _split2(b)
    out = lax.dot_general(a_hi, b_lo, dn, preferred_element_type=F32)
    out = out + lax.dot_general(a_lo, b_hi, dn, preferred_element_type=F32)
    return out + lax.dot_general(a_hi, b_hi, dn, preferred_element_type=F32)


def _cumsum_rows(ltri_bf16, x):
    x1 = x.astype(BF16)
    r1 = x - x1.astype(F32)
    x2 = r1.astype(BF16)
    x3 = (r1 - x2.astype(F32)).astype(BF16)
    dn = _DN['nn']
    out = lax.dot_general(ltri_bf16, x3, dn, preferred_element_type=F32)
    out = out + lax.dot_general(ltri_bf16, x2, dn, preferred_element_type=F32)
    return out + lax.dot_general(ltri_bf16, x1, dn, preferred_element_type=F32)


def _silu(x):
    return x * jax.nn.sigmoid(x)


def _rms(x):
    return x * lax.rsqrt(jnp.mean(x * x, axis=-1, keepdims=True) + EPS)


def _ada_kernel(c_ref, w_ref, b_ref, o_ref):
    cs = _silu(c_ref[...])
    o_ref[...] = _dot(cs, w_ref[...]) + b_ref[...]


def _ada_call(c_all, w_ada, b_ada):
    depth = w_ada.shape[0]
    rows = c_all.shape[0]
    n_out = w_ada.shape[2]
    tn = 1536
    return pl.pallas_call(
        _ada_kernel,
        grid=(depth, n_out // tn),
        in_specs=[
            pl.BlockSpec((rows, D_MODEL), lambda l, j: (0, 0)),
            pl.BlockSpec((None, D_MODEL, tn), lambda l, j: (l, 0, j)),
            pl.BlockSpec((None, 1, tn), lambda l, j: (l, 0, j)),
        ],
        out_specs=pl.BlockSpec((None, rows, tn), lambda l, j: (l, 0, j)),
        out_shape=jax.ShapeDtypeStruct((depth, rows, n_out), F32),
        compiler_params=pltpu.CompilerParams(
            dimension_semantics=("arbitrary", "arbitrary"), vmem_limit_bytes=VMEM_LIMIT),
        name="ada",
    )(c_all, w_ada, b_ada.reshape(depth, 1, n_out))


def _row_tile(n_rows, preferred):
    tm = preferred
    while n_rows % tm:
        tm //= 2
    return tm


def _mod_spec(per_token, tm, tiles_per_seq, seg, ngrid):
    if per_token:
        if ngrid == 2:
            return pl.BlockSpec((tm, D_MODEL), lambda i, j: (i, seg))
        return pl.BlockSpec((tm, D_MODEL), lambda i: (i, seg))
    if ngrid == 2:
        return pl.BlockSpec((None, 1, D_MODEL), lambda i, j: (i // tiles_per_seq, 0, seg))
    return pl.BlockSpec((None, 1, D_MODEL), lambda i: (i // tiles_per_seq, 0, seg))


def _in_kernel(*refs, fuse_conv, tiles_per_seq):
    if fuse_conv:
        (x_ref, sh_ref, sc_ref, nw_ref, w_ref, wab_ref, cw_ref,
         o_ref, g_ref, ab_ref, tail_ref, h_scr, xp_scr, carry_scr) = refs
    else:
        x_ref, sh_ref, sc_ref, nw_ref, w_ref, wab_ref, o_ref, g_ref, ab_ref, h_scr = refs
    i = pl.program_id(0)
    j = pl.program_id(1)
    tm = x_ref.shape[0]

    def conv_group(c, raw):
        lo = c * MIX
        xp_scr[0:SUBLANES, :] = carry_scr[:, lo:lo + MIX]
        xp_scr[SUBLANES:SUBLANES + tm, :] = raw
        conv = xp_scr[SUBLANES:SUBLANES + tm, :] * cw_ref[CONV_W - 1:CONV_W, lo:lo + MIX]
        for t in range(CONV_W - 1):
            off = SUBLANES - (CONV_W - 1) + t
            conv = conv + xp_scr[off:off + tm, :] * cw_ref[t:t + 1, lo:lo + MIX]
        carry_scr[:, lo:lo + MIX] = xp_scr[tm:tm + SUBLANES, :]
        tail_ref[:, lo:lo + MIX] = xp_scr[tm:tm + SUBLANES, :]
        act = _silu(conv)
        if c == 2:
            o_ref[:, 0:MIX] = act
            return
        for hh in range(N_HEADS):
            a = act[:, hh * HEAD:(hh + 1) * HEAD]
            a = a * lax.rsqrt(jnp.sum(a * a, axis=-1, keepdims=True) + EPS)
            if c == 0:
                a = a * (HEAD ** -0.5)
            o_ref[:, hh * HEAD:(hh + 1) * HEAD] = a

    def tile(t, hb):
        for g in range(3):
            epilogue = MIXER_LAYOUT[3 * t + g][2]
            out = jnp.dot(hb, w_ref[t, :, g * MIX:(g + 1) * MIX], preferred_element_type=F32)
            if epilogue == 'conv' and fuse_conv:
                conv_group(t, out)
                continue
            if epilogue == 'silu':
                out = _silu(out)
            elif epilogue == 'sigmoid':
                out = jax.nn.sigmoid(out)
            o_ref[:, g * MIX:(g + 1) * MIX] = out

    @pl.when(j == 0)
    def _():
        h = _rms(x_ref[...]) * nw_ref[...]
        h = h * (1.0 + sc_ref[...]) + sh_ref[...]
        hb = h.astype(BF16)
        h_scr[...] = hb
        ab_ref[...] = jnp.dot(hb, wab_ref[...], preferred_element_type=F32)
        if fuse_conv:
            @pl.when(i % tiles_per_seq == 0)
            def _():
                carry_scr[...] = jnp.zeros_like(carry_scr)
        tile(0, hb)

    for t in range(1, MIXER_TILES):
        @pl.when(j == t)
        def _(t=t):
            tile(t, h_scr[...])

    @pl.when(j >= MIXER_TILES)
    def _():
        acc = jnp.dot(h_scr[...], w_ref[j], preferred_element_type=F32)
        g_ref[...] = jax.nn.sigmoid(acc).astype(BF16)


def _in_call(x2, mod, per_token, seq_len, norm_w, w_main, w_ab, conv_w, fuse_conv, layer):
    T = x2.shape[0]
    tm = _row_tile(T if per_token else seq_len, 512 if per_token else 1024)
    tn = IN_TN
    tps = max(seq_len // tm, 1)
    last_gate = IN_TILES - MIXER_TILES - 1
    assert not fuse_conv or not per_token
    assert all(MIXER_LAYOUT[3 * c][0] == name for c, name in enumerate(CONV_GROUPS))

    def gate_block(i, j):
        on_gate = j >= MIXER_TILES
        row = jnp.where(on_gate, i, jnp.maximum(i - 1, 0))
        col = jnp.where(on_gate, j - MIXER_TILES, jnp.where(i > 0, last_gate, 0))
        return row, col

    in_specs = [
        pl.BlockSpec((tm, D_MODEL), lambda i, j: (i, 0)),
        _mod_spec(per_token, tm, tps, 0, 2),
        _mod_spec(per_token, tm, tps, 1, 2),
        pl.BlockSpec((1, D_MODEL), lambda i, j: (0, 0)),
        pl.BlockSpec((None, IN_TILES, D_MODEL, tn), lambda i, j: (layer, 0, 0, 0), pipeline_mode=pl.Buffered(1)),
        pl.BlockSpec((None, D_MODEL, AB_COLS), lambda i, j: (layer, 0, 0)),
    ]
    out_specs = [
        pl.BlockSpec((tm, tn), lambda i, j: (i, jnp.minimum(j, MIXER_TILES - 1))),
        pl.BlockSpec((tm, tn), gate_block),
        pl.BlockSpec((tm, AB_COLS), lambda i, j: (i, 0)),
    ]
    out_shape = [
        jax.ShapeDtypeStruct((T, MIXER_COLS), F32),
        jax.ShapeDtypeStruct((T, N_BRANCH * D_MODEL), BF16),
        jax.ShapeDtypeStruct((T, AB_COLS), F32),
    ]
    scratch = [pltpu.VMEM((tm, D_MODEL), BF16)]
    operands = [x2, mod, mod, norm_w.reshape(1, D_MODEL), w_main, w_ab]
    if fuse_conv:
        in_specs.append(pl.BlockSpec((CONV_W, CONV_DIM), lambda i, j: (0, 0)))
        operands.append(conv_w)
        out_specs.append(pl.BlockSpec((SUBLANES, CONV_DIM), lambda i, j: (i, 0)))
        out_shape.append(jax.ShapeDtypeStruct((T // tm * SUBLANES, CONV_DIM), F32))
        scratch += [pltpu.VMEM((tm + SUBLANES, MIX), F32), pltpu.VMEM((SUBLANES, CONV_DIM), F32)]
    return pl.pallas_call(
        functools.partial(_in_kernel, fuse_conv=fuse_conv, tiles_per_seq=tps),
        grid=(T // tm, IN_TILES),
        in_specs=in_specs,
        out_specs=out_specs,
        out_shape=out_shape,
        scratch_shapes=scratch,
        compiler_params=pltpu.CompilerParams(
            dimension_semantics=("arbitrary", "arbitrary"), vmem_limit_bytes=VMEM_LIMIT),
        name="in_proj",
    )(*operands)


def _mixer_kernel(*refs, C, Lv, nb, layer, passes, has_state, depth, pre_conv):
    proj_ref, ab_ref = refs[:2]
    pos = 2
    if has_state:
        conv0_ref, sg0_ref, sr0_ref, sh0_ref = refs[pos:pos + 4]
        pos += 4
    (cw_ref, hp_ref, gn_ref, hn_ref, lb_ref, cos_ref, sin_ref,
     dtab_ref, inner_ref, toend_ref, cdec_ref) = refs[pos:pos + 11]
    pos += 11 + (3 if layer > 0 else 0)
    ob_ref, sg_ref, sr_ref, sh_ref, hg_ref = refs[pos:pos + 5]
    xpad_ref = None if pre_conv else refs[-1]
    stacked = (sg_ref, sr_ref, sh_ref)
    if layer == 0:
        sg_ref, sr_ref, sh_ref = (r.at[0] for r in stacked)
    tail0 = SUBLANES - (CONV_W - 1)

    @pl.when(pl.program_id(1) == 0)
    def _():
        if layer == 0 and depth > 1:
            for r in stacked:
                r[1:] = jnp.zeros((depth - 1,) + r.shape[1:], F32)
        if has_state:
            sg_ref[...] = sg0_ref[...]
            sr_ref[...] = sr0_ref[...]
            sh_ref[...] = sh0_ref[...]
            xpad_ref[:, tail0:SUBLANES, :] = conv0_ref[...]
        else:
            sg_ref[...] = jnp.zeros_like(sg_ref)
            sr_ref[...] = jnp.zeros_like(sr_ref)
            sh_ref[...] = jnp.zeros_like(sh_ref)
            if not pre_conv:
                xpad_ref[:, tail0:SUBLANES, :] = jnp.zeros((nb, CONV_W - 1, CONV_DIM), F32)

    padded = Lv < C
    row1 = lax.broadcasted_iota(jnp.int32, (C, 1), 0)
    valid = row1 < Lv
    ri = lax.broadcasted_iota(jnp.int32, (C, C), 0)
    ci = lax.broadcasted_iota(jnp.int32, (C, C), 1)
    incl = ri >= ci
    strict = ri > ci
    eye = ri == ci
    ltri = jnp.where(incl, 1.0, 0.0).astype(BF16)
    r128 = lax.broadcasted_iota(jnp.int32, (HEAD, HEAD), 0)
    c128 = lax.broadcasted_iota(jnp.int32, (HEAD, HEAD), 1)
    eye128 = r128 == c128
    n_levels = max((Lv - 1).bit_length(), 1)

    lbp = lb_ref[...]
    lbe = jnp.exp(lbp - jnp.max(lbp, axis=0, keepdims=True))
    lbw = lbe / jnp.sum(lbe, axis=0, keepdims=True)
    lb_row = lbw[0:1, :]
    for d in range(1, layer + 1):
        lb_row = lb_row + lbw[d:d + 1, :]
    lb_row = lb_row - lbw[0:1, :]

    neg_a = -jnp.exp(hp_ref[0:1, :])
    dt_b = hp_ref[1:2, :]
    gn = gn_ref[...]
    hn = hn_ref[...]
    cosf = cos_ref[...]
    sinf = sin_ref[...]

    SB = SUBLANES

    def head_cols(n, name, h):
        return proj_ref[n, :, COL[name] + h * HEAD:COL[name] + (h + 1) * HEAD]

    def gdn_task(n, h, qkv, G_all, beta_all):
        if pre_conv:
            q = head_cols(n, 'gdn_q', h)
            k = head_cols(n, 'gdn_k', h)
            v = head_cols(n, 'gdn_v', h)
        else:
            q = qkv[:, h * HEAD:(h + 1) * HEAD]
            k = qkv[:, MIX + h * HEAD:MIX + (h + 1) * HEAD]
            v = qkv[:, 2 * MIX + h * HEAD:2 * MIX + (h + 1) * HEAD]
            q = q * lax.rsqrt(jnp.sum(q * q, axis=-1, keepdims=True) + EPS) * (HEAD ** -0.5)
            k = k * lax.rsqrt(jnp.sum(k * k, axis=-1, keepdims=True) + EPS)
        kk = _dot(k, k, 'nt', passes)
        qk = _dot(q, k, 'nt', passes)
        yield
        Gc = G_all[:, h:h + 1]
        beta = beta_all[:, N_HEADS + h:N_HEADS + h + 1]
        Gr = jnp.sum(jnp.where(eye, Gc, 0.0), axis=0, keepdims=True)
        G_last = G_all[C - 1:C, h:h + 1]
        eG = jnp.exp(Gc)
        decay = jnp.where(incl, jnp.exp(jnp.where(incl, Gc - Gr, 0.0)), 0.0)
        P = -jnp.where(strict, beta * kk * decay, 0.0)
        sol = jnp.concatenate([beta * v, (beta * eG) * k], axis=-1)
        for lvl in range(n_levels):
            upd = _dot(P, sol, 'nn', passes)
            if lvl + 1 < n_levels:
                P = _dot(P, P, 'nn', passes)
            yield
            sol = sol + upd
        S = sg_ref[n, h]
        sk_s = _dot(sol[:, HEAD:], S, 'nn', passes)
        q_s = _dot(q, S, 'nn', passes)
        yield
        u = sol[:, :HEAD] - sk_s
        k_end = k * jnp.exp(G_last - Gc)
        intra = _dot(qk * decay, u, 'nn', passes)
        s_upd = _dot(k_end, u, 'tn', passes)
        yield
        o = eG * q_s + intra
        sg_ref[n, h] = jnp.exp(G_last) * S + s_upd
        z_act = head_cols(n, 'gdn_z', h)
        ob_ref[n, :, h * HEAD:(h + 1) * HEAD] = ((_rms(o) * gn) * z_act).astype(ob_ref.dtype)

    def ret_task(n, h):
        q = head_cols(n, 'ret_q', h)
        k = head_cols(n, 'ret_k', h)
        v = head_cols(n, 'ret_v', h)
        q = q * cosf + pltpu.roll(q, HEAD // 2, 1) * sinf
        k = (k * cosf + pltpu.roll(k, HEAD // 2, 1) * sinf) * (HEAD ** -0.5)
        if padded:
            v = jnp.where(valid, v, 0.0)
        S = sr_ref[n, h]
        qk = _dot(q, k, 'nt', passes)
        q_s = _dot(q, S, 'nn', passes)
        s_upd = _dot(k * toend_ref[h], v, 'tn', passes)
        yield
        intra = _dot(qk * dtab_ref[h], v, 'nn', passes)
        sr_ref[n, h] = cdec_ref[h] * S + s_upd
        yield
        o = intra + inner_ref[h] * q_s
        g_act = head_cols(n, 'ret_g', h)
        mu = jnp.mean(o, axis=-1, keepdims=True)
        oc = o - mu
        var = jnp.mean(oc * oc, axis=-1, keepdims=True)
        ob_ref[n, :, MIX + h * HEAD:MIX + (h + 1) * HEAD] = (
            oc * lax.rsqrt(var + EPS) * g_act).astype(ob_ref.dtype)

    def hg_task(n, h):
        qc = head_cols(n, 'hg_q', h)
        zf = head_cols(n, 'hg_f', h)
        vc = head_cols(n, 'hg_i', h)
        lb_h = lb_row[:, h * HEAD:(h + 1) * HEAD]
        s_neg = jax.nn.sigmoid(-zf)
        f_gate = jax.nn.sigmoid(zf) + lb_h * s_neg
        log_f = jnp.log(jnp.maximum(f_gate, F_TINY))
        kc = (1.0 - lb_h) * s_neg
        if padded:
            log_f = jnp.where(valid, log_f, 0.0)
            kc = jnp.where(valid, kc, 0.0)
            vc = jnp.where(valid, vc, 0.0)
        A = _cumsum_rows(ltri, log_f)
        yield
        hgs = hg_ref.at[n, h]
        hgs[0] = A
        hgs[1] = kc
        S = sh_ref[n, h]
        a_last = hgs[0, C - 1:C, :]
        q_s = _dot(qc * jnp.exp(A), S, 'nn', passes)
        s_upd = _dot(kc * jnp.exp(a_last - A), vc, 'tn', passes)
        score_rows = []
        for blk in range(C // SB):
            r0 = blk * SB
            qI = qc[r0:r0 + SB]
            AI = A[r0:r0 + SB]
            rowg = r0 + lax.broadcasted_iota(jnp.int32, (SB, C), 0)
            colg = lax.broadcasted_iota(jnp.int32, (SB, C), 1)
            cross = None
            if r0 > 0:
                a_b = hgs[0, r0 - 1:r0, :]
                kt = jnp.concatenate([kc[:r0] * jnp.exp(a_b - A[:r0]), jnp.zeros((C - r0, HEAD), F32)], axis=0)
                qt = qI * jnp.exp(AI - a_b)
                cross = _dot(qt, kt, 'nt', passes)
            sc_blk = jnp.zeros((SB, C), F32)
            for s in range(SB):
                a_s = hgs[0, r0 + s:r0 + s + 1, :]
                k_s = hgs[1, r0 + s:r0 + s + 1, :]
                w = jnp.exp(AI - a_s)
                val = jnp.sum(qI * k_s * w, axis=-1, keepdims=True)
                sc_blk = jnp.where(colg == r0 + s, val, sc_blk)
            yield
            if cross is not None:
                sc_blk = sc_blk + cross
            score_rows.append(jnp.where(rowg >= colg, sc_blk, 0.0))
        scores = score_rows[0] if len(score_rows) == 1 else jnp.concatenate(score_rows, axis=0)
        intra = _dot(scores, vc, 'nn', passes)
        dec_col = jnp.sum(jnp.where(eye128, jnp.exp(a_last), 0.0), axis=1, keepdims=True)
        sh_ref[n, h] = dec_col * S + s_upd
        yield
        o = intra + q_s
        g_act = head_cols(n, 'hg_g', h)
        ob_ref[n, :, 2 * MIX + h * HEAD:2 * MIX + (h + 1) * HEAD] = (
            (_rms(o) * hn) * g_act).astype(ob_ref.dtype)

    stagger = STAGGER_ROUNDS if nb <= 2 else 0
    gdn_tasks, other_tasks = [], []
    for n in range(nb):
        qkv = None
        if not pre_conv:
            for c, name in enumerate(CONV_GROUPS):
                xpad_ref[n, SUBLANES:SUBLANES + C, c * MIX:(c + 1) * MIX] = proj_ref[n, :, COL[name]:COL[name] + MIX]
            conv = xpad_ref[n, SUBLANES:SUBLANES + C, :] * cw_ref[CONV_W - 1:CONV_W, :]
            for j in range(CONV_W - 1):
                off = SUBLANES - (CONV_W - 1) + j
                conv = conv + xpad_ref[n, off:off + C, :] * cw_ref[j:j + 1, :]
            xpad_ref[n, 0:SUBLANES, :] = xpad_ref[n, C:C + SUBLANES, :]
            qkv = _silu(conv)
        ab = ab_ref[n]
        g_all = neg_a * jax.nn.softplus(ab + dt_b)
        beta_all = jax.nn.sigmoid(ab)
        if padded:
            g_all = jnp.where(valid, g_all, 0.0)
            beta_all = jnp.where(valid, beta_all, 0.0)
        G_all = _cumsum_rows(ltri, g_all)
        for h in range(N_HEADS):
            gdn_tasks.append([n * stagger, gdn_task(n, h, qkv, G_all, beta_all)])
            other_tasks.append([n * stagger, hg_task(n, h)])
            other_tasks.append([n * stagger, ret_task(n, h)])
    tasks = gdn_tasks + other_tasks
    while tasks:
        alive = []
        for t in tasks:
            if t[0] > 0:
                t[0] -= 1
                alive.append(t)
                continue
            try:
                next(t[1])
                alive.append(t)
            except StopIteration:
                pass
        tasks = alive


def _mixer_call(proj3, ab3, states, stacked_so_far, conv_w, head_params, gdn_norm, hg_norm, hgrn_lb,
                cosf, sinf, dtab, inner, toend, cdec, *, C, Lv, nb, layer, passes, pre_conv):
    B, Lp, _ = proj3.shape
    n_chunks = Lp // C
    depth = hgrn_lb.shape[0]
    in_state_spec = pl.BlockSpec((None, nb, N_HEADS, HEAD, HEAD), lambda b, i: (layer, b, 0, 0, 0))
    if layer == 0:
        out_state_spec = pl.BlockSpec((depth, nb, N_HEADS, HEAD, HEAD), lambda b, i: (0, b, 0, 0, 0))
    else:
        out_state_spec = in_state_spec
    state_shape = jax.ShapeDtypeStruct((depth, B, N_HEADS, HEAD, HEAD), F32)
    ob_dtype = BF16 if C % (2 * SUBLANES) == 0 else F32

    def whole(shape):
        return pl.BlockSpec(shape, lambda b, i: (0,) * len(shape))

    has_state = states is not None
    operands = [proj3, ab3]
    in_specs = [
        pl.BlockSpec((nb, C, MIXER_COLS), lambda b, i: (b, i, 0)),
        pl.BlockSpec((nb, C, AB_COLS), lambda b, i: (b, i, 0)),
    ]
    if has_state:
        operands += [states[1], states[0], states[2], states[3]]
        in_specs += [pl.BlockSpec((None, nb, CONV_W - 1, CONV_DIM), lambda b, i: (layer, b, 0, 0)),
                     in_state_spec, in_state_spec, in_state_spec]
    operands += [conv_w, head_params, gdn_norm.reshape(1, HEAD), hg_norm.reshape(1, HEAD), hgrn_lb,
                 cosf, sinf, dtab, inner, toend, cdec]

    aliased = list(stacked_so_far) if layer > 0 else []
    aliases = {len(operands) + k: 1 + k for k in range(len(aliased))}
    kern = functools.partial(_mixer_kernel, C=C, Lv=Lv, nb=nb, layer=layer, passes=passes,
                             has_state=has_state, depth=depth, pre_conv=pre_conv)
    scratch = [pltpu.VMEM((nb, N_HEADS, 2, C, HEAD), F32)]
    if not pre_conv:
        scratch.append(pltpu.VMEM((nb, C + SUBLANES, CONV_DIM), F32))
    return pl.pallas_call(
        kern,
        grid=(B // nb, n_chunks),
        in_specs=in_specs + [
            whole((CONV_W, CONV_DIM)),
            whole((2, AB_COLS)),
            whole((1, HEAD)),
            whole((1, HEAD)),
            whole((depth, MIX)),
            pl.BlockSpec((C, HEAD), lambda b, i: (i, 0)),
            pl.BlockSpec((C, HEAD), lambda b, i: (i, 0)),
            whole((N_HEADS, C, C)),
            whole((N_HEADS, C, HEAD)),
            whole((N_HEADS, C, HEAD)),
            whole((N_HEADS, 1, HEAD)),
        ] + [pl.BlockSpec(memory_space=pl.ANY)] * len(aliased),
        out_specs=[
            pl.BlockSpec((nb, C, N_BRANCH * MIX), lambda b, i: (b, i, 0)),
            out_state_spec, out_state_spec, out_state_spec,
        ],
        out_shape=[
            jax.ShapeDtypeStruct((B, Lp, N_BRANCH * MIX), ob_dtype),
            state_shape, state_shape, state_shape,
        ],
        input_output_aliases=aliases,
        scratch_shapes=scratch,
        compiler_params=pltpu.CompilerParams(
            dimension_semantics=("arbitrary", "arbitrary"), vmem_limit_bytes=VMEM_LIMIT),
        name="mixer",
    )(*operands, *aliased)


FF_SPLITS = (0, 1024, 2048, D_FF)


def _post_ffn_kernel(ob_ref, mg_ref, x_ref, gtm_ref, sh_ref, sc_ref, gtf_ref, nw_ref, fw_ref,
                     wb_ref, wo_ref, wg_ref, wu_ref, wd_ref, o_ref, *, final):
    merged = None
    for n in range(N_BRANCH):
        br = _dot(ob_ref[:, n * MIX:(n + 1) * MIX], wb_ref[n])
        term = mg_ref[:, n * D_MODEL:(n + 1) * D_MODEL].astype(F32) * br
        merged = term if merged is None else merged + term
    x = x_ref[...] + gtm_ref[...] * _dot(merged, wo_ref[...])
    h = (_rms(x) * nw_ref[...] * (1.0 + sc_ref[...]) + sh_ref[...]).astype(BF16)
    y = None
    for lo, hi in zip(FF_SPLITS[:-1], FF_SPLITS[1:]):
        g = jnp.dot(h, wg_ref[:, lo:hi], preferred_element_type=F32)
        u = jnp.dot(h, wu_ref[:, lo:hi], preferred_element_type=F32)
        part = _dot(_silu(g) * u, wd_ref[lo:hi, :])
        y = part if y is None else y + part
    xn = x + gtf_ref[...] * y
    if final:
        xn = _rms(xn) * fw_ref[...]
    o_ref[...] = xn


def _post_ffn_call(ob2, gates, x2, mod, per_token, seq_len, norm_w, final_w, w_branch, w_out, w_gate_up, w_down,
                   final, layer):
    T = x2.shape[0]
    tm = _row_tile(T if per_token else seq_len, 256 if per_token else 512)
    tps = max(seq_len // tm, 1)
    resident = pl.Buffered(1)
    return pl.pallas_call(
        functools.partial(_post_ffn_kernel, final=final),
        grid=(T // tm,),
        in_specs=[
            pl.BlockSpec((tm, N_BRANCH * MIX), lambda i: (i, 0)),
            pl.BlockSpec((tm, N_BRANCH * D_MODEL), lambda i: (i, 0)),
            pl.BlockSpec((tm, D_MODEL), lambda i: (i, 0)),
            _mod_spec(per_token, tm, tps, 2, 1),
            _mod_spec(per_token, tm, tps, 3, 1),
            _mod_spec(per_token, tm, tps, 4, 1),
            _mod_spec(per_token, tm, tps, 5, 1),
            pl.BlockSpec((1, D_MODEL), lambda i: (0, 0)),
            pl.BlockSpec((1, D_MODEL), lambda i: (0, 0)),
            pl.BlockSpec((None, N_BRANCH, MIX, D_MODEL), lambda i: (layer, 0, 0, 0), pipeline_mode=resident),
            pl.BlockSpec((None, D_MODEL, D_MODEL), lambda i: (layer, 0, 0), pipeline_mode=resident),
            pl.BlockSpec((None, D_MODEL, D_FF), lambda i: (layer, 0, 0), pipeline_mode=resident),
            pl.BlockSpec((None, D_MODEL, D_FF), lambda i: (layer, 0, 1), pipeline_mode=resident),
            pl.BlockSpec((None, D_FF, D_MODEL), lambda i: (layer, 0, 0), pipeline_mode=resident),
        ],
        out_specs=pl.BlockSpec((tm, D_MODEL), lambda i: (i, 0)),
        out_shape=jax.ShapeDtypeStruct((T, D_MODEL), F32),
        compiler_params=pltpu.CompilerParams(
            dimension_semantics=("arbitrary",), vmem_limit_bytes=VMEM_LIMIT),
        name="post_ffn",
    )(ob2, gates, x2, mod, mod, mod, mod, norm_w.reshape(1, D_MODEL), final_w.reshape(1, D_MODEL),
      w_branch, w_out, w_gate_up, w_gate_up, w_down)


def _rope_tables(pos0, length, padded_len):
    half = HEAD // 2
    pos = pos0 + jnp.arange(padded_len, dtype=jnp.int32)
    inv = ROPE_BASE ** (-jnp.arange(half, dtype=F32) / half)
    ang = pos.astype(F32)[:, None] * inv[None, :]
    cos, sin = jnp.cos(ang), jnp.sin(ang)
    return jnp.concatenate([cos, cos], axis=-1), jnp.concatenate([-sin, sin], axis=-1)


def _retention_tables(c_real, c_pad):
    log_gamma = jnp.log1p(-jnp.exp2(-5.0 - jnp.arange(N_HEADS, dtype=F32)))
    idx = jnp.arange(c_real, dtype=F32)
    rel = idx[:, None] - idx[None, :]
    mask = rel >= 0
    lg = log_gamma[:, None, None]
    dtab = jnp.where(mask, jnp.exp(jnp.where(mask, lg * rel, 0.0)), 0.0)
    inner = jnp.exp(log_gamma[:, None] * (idx + 1.0))
    to_end = jnp.exp(log_gamma[:, None] * (c_real - 1.0 - idx))
    cdec = jnp.exp(log_gamma * c_real)
    p = c_pad - c_real
    dtab = jnp.pad(dtab, ((0, 0), (0, p), (0, p)))
    inner = jnp.broadcast_to(jnp.pad(inner, ((0, 0), (0, p)))[:, :, None], (N_HEADS, c_pad, HEAD))
    to_end = jnp.broadcast_to(jnp.pad(to_end, ((0, 0), (0, p)))[:, :, None], (N_HEADS, c_pad, HEAD))
    cdec = jnp.broadcast_to(cdec[:, None, None], (N_HEADS, 1, HEAD))
    return dtab, inner, to_end, cdec


def _trunk(x, mod_all, states, pos0, wts, nb, passes):
    B, L, _ = x.shape
    depth = mod_all.shape[0]
    c_real = CHUNK if L % CHUNK == 0 else L
    assert L % c_real == 0 and c_real <= CHUNK
    C = -(-c_real // SUBLANES) * SUBLANES
    Lp = L if C == c_real else C
    assert Lp == L or L == c_real
    per_token = Lp < 256
    fuse_conv = states is None and not per_token
    if Lp != L:
        x = jnp.pad(x, ((0, 0), (0, Lp - L), (0, 0)))
    T = B * Lp
    x2 = x.reshape(T, D_MODEL)
    cosf, sinf = _rope_tables(pos0, L, Lp)
    dtab, inner, toend, cdec = _retention_tables(c_real, C)

    tails, stacked = [], None
    for l in range(depth):
        last = l == depth - 1
        if per_token:
            mod = jnp.repeat(mod_all[l], Lp, axis=0)
        else:
            mod = mod_all[l].reshape(B, 1, 6 * D_MODEL)
        proj, gates, ab, *raw_tail = _in_call(x2, mod, per_token, Lp, wts['norm_mix'][l], wts['w_main'],
                                              wts['w_ab'], wts['conv_w'][l], fuse_conv, l)
        proj3 = proj.reshape(B, Lp, MIXER_COLS)
        ob, *stacked = _mixer_call(
            proj3, ab.reshape(B, Lp, AB_COLS), states, stacked,
            wts['conv_w'][l], wts['head_params'][l], wts['gdn_norm'][l], wts['hgrn_norm'][l], wts['hgrn_lb'],
            cosf, sinf, dtab, inner, toend, cdec, C=C, Lv=c_real, nb=nb, layer=l,
            passes=passes, pre_conv=fuse_conv)
        if fuse_conv:
            tail = raw_tail[0].reshape(B, -1, SUBLANES, CONV_DIM)[:, -1, SUBLANES - (CONV_W - 1):]
        else:
            tail = jnp.concatenate([proj3[:, L - min(L, CONV_W - 1):L, COL[name]:COL[name] + MIX]
                                    for name in CONV_GROUPS], axis=-1)
            if L < CONV_W - 1:
                tail = jnp.concatenate([states[1][l][:, L:], tail], axis=1)
        tails.append(tail)
        x2 = _post_ffn_call(ob.reshape(T, N_BRANCH * MIX), gates, x2, mod, per_token, Lp, wts['norm_ffn'][l],
                            wts['final_norm'], wts['w_branch'], wts['w_out'], wts['w_gate_up'],
                            wts['w_down'], final=last, layer=l)
    y = x2.reshape(B, Lp, D_MODEL)[:, :L]
    sg, sr, sh = stacked
    return y, (sg, jnp.stack(tails), sr, sh)


LANES = 128
GROUP_SRC = tuple(src for _, src, _ in MIXER_LAYOUT) + tuple(
    MERGE_SRC + k * MIX for k in range(N_BRANCH * D_MODEL // MIX))


def _regroup_kernel(a_tab, sh_tab, a_ref, b_ref, o_ref):
    del a_tab
    shifted = sh_tab[pl.program_id(1)] != 0

    @pl.when(shifted)
    def _():
        wide = jnp.concatenate([a_ref[...], b_ref[...]], axis=-1).astype(F32)
        skip = 2 * N_HEADS
        o_ref[...] = pltpu.roll(wide, MIX + LANES - skip, 1)[:, :MIX].astype(BF16)

    @pl.when(jnp.logical_not(shifted))
    def _():
        o_ref[...] = a_ref[...]


def _regroup_w_in(w_in):
    depth = w_in.shape[0]
    n_groups = len(GROUP_SRC)
    a_tab = jnp.asarray([src // MIX for src in GROUP_SRC], jnp.int32)
    sh_tab = jnp.asarray([src % MIX != 0 for src in GROUP_SRC], jnp.int32)
    assert all(src % MIX in (0, 2 * N_HEADS) for src in GROUP_SRC)
    per_tile = IN_TN // MIX
    grid_spec = pltpu.PrefetchScalarGridSpec(
        num_scalar_prefetch=2,
        grid=(depth, n_groups),
        in_specs=[
            pl.BlockSpec((None, D_MODEL, MIX), lambda l, g, a, s: (l, 0, a[g])),
            pl.BlockSpec((None, D_MODEL, LANES), lambda l, g, a, s: (l, 0, (a[g] + 1) * (MIX // LANES))),
        ],
        out_specs=pl.BlockSpec((None, None, D_MODEL, MIX), lambda l, g, a, s: (l, g // per_tile, 0, g % per_tile)),
    )
    return pl.pallas_call(
        _regroup_kernel,
        grid_spec=grid_spec,
        out_shape=jax.ShapeDtypeStruct((depth, IN_TILES, D_MODEL, IN_TN), BF16),
        compiler_params=pltpu.CompilerParams(
            dimension_semantics=("arbitrary", "arbitrary"), vmem_limit_bytes=VMEM_LIMIT),
        name="regroup_w_in",
    )(a_tab, sh_tab, w_in, w_in)


def _prep_weights(w_in, conv_w, gdn_a_log, gdn_dt_bias, gdn_norm, hgrn_lb, hgrn_norm, w_branch, w_out,
                  norm_mix, norm_ffn, w_gate_up, w_down, final_norm):
    w_main = _regroup_w_in(w_in.astype(BF16))
    w_ab = jnp.pad(w_in[:, :, AB_SRC:AB_SRC + 2 * N_HEADS],
                   ((0, 0), (0, 0), (0, AB_COLS - 2 * N_HEADS))).astype(BF16)
    head_params = jnp.pad(jnp.stack([gdn_a_log, gdn_dt_bias], axis=1),
                          ((0, 0), (0, 0), (0, AB_COLS - N_HEADS))).astype(F32)
    return dict(
        w_main=w_main, w_ab=w_ab, conv_w=conv_w, head_params=head_params, gdn_norm=gdn_norm,
        hgrn_norm=hgrn_norm, hgrn_lb=hgrn_lb, w_branch=w_branch.astype(BF16), w_out=w_out.astype(BF16),
        norm_mix=norm_mix, norm_ffn=norm_ffn, w_gate_up=w_gate_up.astype(BF16),
        w_down=w_down.astype(BF16), final_norm=final_norm)


def kernel(x_prompt, x_sample, state_gdn, state_gdn_conv, state_ret, state_hgrn, c_prompt, c_sample,
           w_in, conv_w, gdn_a_log, gdn_dt_bias, gdn_norm, hgrn_lb, hgrn_norm, w_branch, w_out,
           w_ada, b_ada, norm_mix, norm_ffn, w_gate_up, w_down, final_norm):
    Bp = x_prompt.shape[0]
    wts = _prep_weights(w_in, conv_w, gdn_a_log, gdn_dt_bias, gdn_norm, hgrn_lb, hgrn_norm, w_branch, w_out,
                        norm_mix, norm_ffn, w_gate_up, w_down, final_norm)
    c_all = jnp.concatenate([c_prompt, c_sample], axis=0)
    mod_all = _ada_call(c_all, w_ada, b_ada)
    y_p, st_p = _trunk(x_prompt, mod_all[:, :Bp], None, 0, wts, nb=2, passes=1)
    y_s, st_s = _trunk(x_sample, mod_all[:, Bp:], (state_gdn, state_gdn_conv, state_ret, state_hgrn),
                       PAST_LEN, wts, nb=8, passes=1)
    return (y_p, y_s) + st_p + st_s
```

```python
import functools

import jax
import jax.numpy as jnp
from jax import lax
from jax.experimental import pallas as pl
from jax.experimental.pallas import tpu as pltpu

F32 = jnp.float32
BF16 = jnp.bfloat16

D_MODEL = 1024
N_HEADS = 4
HEAD = 128
MIX = N_HEADS * HEAD
CONV_W = 4
CONV_DIM = 3 * MIX
CHUNK = 128
N_BRANCH = 3
D_FF = 2816
ROPE_BASE = 10000.0
EPS = 1e-6
F_TINY = 1e-30
PAST_LEN = 16384

IN_TN = 3 * MIX
MIXER_LAYOUT = (
    ('gdn_q', 0, 'conv'), ('gdn_z', 3 * MIX, 'silu'), ('ret_q', 4 * MIX + 8, None),
    ('gdn_k', MIX, 'conv'), ('ret_k', 5 * MIX + 8, None), ('ret_v', 6 * MIX + 8, None),
    ('gdn_v', 2 * MIX, 'conv'), ('ret_g', 7 * MIX + 8, 'silu'), ('hg_q', 8 * MIX + 8, None),
    ('hg_f', 9 * MIX + 8, None), ('hg_i', 10 * MIX + 8, None), ('hg_g', 11 * MIX + 8, 'sigmoid'),
)
COL = {name: idx * MIX for idx, (name, _, _) in enumerate(MIXER_LAYOUT)}
MIXER_COLS = len(MIXER_LAYOUT) * MIX
MERGE_SRC = 12 * MIX + 8
MAIN_COLS = MIXER_COLS + N_BRANCH * D_MODEL
AB_COLS = 128
AB_SRC = 4 * MIX
IN_TILES = MAIN_COLS // IN_TN
MIXER_TILES = MIXER_COLS // IN_TN
CONV_GROUPS = ('gdn_q', 'gdn_k', 'gdn_v')

SUBLANES = 8
STAGGER_ROUNDS = 4
VMEM_LIMIT = 56 * 1024 * 1024

_DN = {
    'nn': (((1,), (0,)), ((), ())),
    'nt': (((1,), (1,)), ((), ())),
    'tn': (((0,), (0,)), ((), ())),
}


def _split2(a):
    hi = a.astype(BF16)
    lo = (a - hi.astype(F32)).astype(BF16)
    return hi, lo


def _dot(a, b, dims='nn', passes=1):
    dn = _DN[dims]
    if passes == 1:
        return lax.dot_general(a.astype(BF16), b.astype(BF16), dn, preferred_element_type=F32)
    a_hi, a_lo = _split2(a)
    b_hi, b_lo = _split2(b)
    out = lax.dot_general(a_hi, b_lo, dn, preferred_element_type=F32)
    out = out + lax.dot_general(a_lo, b_hi, dn, preferred_element_type=F32)
    return out + lax.dot_general(a_hi, b_hi, dn, preferred_element_type=F32)


def _cumsum_rows(ltri_bf16, x):
    x1 = x.astype(BF16)
    r1 = x - x1.astype(F32)
    x2 = r1.astype(BF16)
    x3 = (r1 - x2.astype(F32)).astype(BF16)
    dn = _DN['nn']
    out = lax.dot_general(ltri_bf16, x3, dn, preferred_element_type=F32)
    out = out + lax.dot_general(ltri_bf16, x2, dn, preferred_element_type=F32)
    return out + lax.dot_general(ltri_bf16, x1, dn, preferred_element_type=F32)


def _silu(x):
    return x * jax.nn.sigmoid(x)


def _rms(x):
    return x * lax.rsqrt(jnp.mean(x * x, axis=-1, keepdims=True) + EPS)


def _ada_kernel(c_ref, w_ref, b_ref, o_ref):
    cs = _silu(c_ref[...])
    o_ref[...] = _dot(cs, w_ref[...]) + b_ref[...]


def _ada_call(c_all, w_ada, b_ada):
    depth = w_ada.shape[0]
    rows = c_all.shape[0]
    n_out = w_ada.shape[2]
    tn = 1536
    return pl.pallas_call(
        _ada_kernel,
        grid=(depth, n_out // tn),
        in_specs=[
            pl.BlockSpec((rows, D_MODEL), lambda l, j: (0, 0)),
            pl.BlockSpec((None, D_MODEL, tn), lambda l, j: (l, 0, j)),
            pl.BlockSpec((None, 1, tn), lambda l, j: (l, 0, j)),
        ],
        out_specs=pl.BlockSpec((None, rows, tn), lambda l, j: (l, 0, j)),
        out_shape=jax.ShapeDtypeStruct((depth, rows, n_out), F32),
        compiler_params=pltpu.CompilerParams(
            dimension_semantics=("arbitrary", "arbitrary"), vmem_limit_bytes=VMEM_LIMIT),
        name="ada",
    )(c_all, w_ada, b_ada.reshape(depth, 1, n_out))


def _row_tile(n_rows, preferred):
    tm = preferred
    while n_rows % tm:
        tm //= 2
    return tm


def _mod_spec(per_token, tm, tiles_per_seq, seg, ngrid):
    if per_token:
        if ngrid == 2:
            return pl.BlockSpec((tm, D_MODEL), lambda i, j: (i, seg))
        return pl.BlockSpec((tm, D_MODEL), lambda i: (i, seg))
    if ngrid == 2:
        return pl.BlockSpec((None, 1, D_MODEL), lambda i, j: (i // tiles_per_seq, 0, seg))
    return pl.BlockSpec((None, 1, D_MODEL), lambda i: (i // tiles_per_seq, 0, seg))


def _in_kernel(*refs, fuse_conv, tiles_per_seq):
    if fuse_conv:
        (x_ref, sh_ref, sc_ref, nw_ref, w_ref, wab_ref, cw_ref,
         o_ref, g_ref, ab_ref, tail_ref, h_scr, xp_scr, carry_scr) = refs
    else:
        x_ref, sh_ref, sc_ref, nw_ref, w_ref, wab_ref, o_ref, g_ref, ab_ref, h_scr = refs
    i = pl.program_id(0)
    j = pl.program_id(1)
    tm = x_ref.shape[0]

    def conv_group(c, raw):
        lo = c * MIX
        xp_scr[0:SUBLANES, :] = carry_scr[:, lo:lo + MIX]
        xp_scr[SUBLANES:SUBLANES + tm, :] = raw
        conv = xp_scr[SUBLANES:SUBLANES + tm, :] * cw_ref[CONV_W - 1:CONV_W, lo:lo + MIX]
        for t in range(CONV_W - 1):
            off = SUBLANES - (CONV_W - 1) + t
            conv = conv + xp_scr[off:off + tm, :] * cw_ref[t:t + 1, lo:lo + MIX]
        carry_scr[:, lo:lo + MIX] = xp_scr[tm:tm + SUBLANES, :]
        tail_ref[:, lo:lo + MIX] = xp_scr[tm:tm + SUBLANES, :]
        act = _silu(conv)
        if c == 2:
            o_ref[:, 0:MIX] = act
            return
        for hh in range(N_HEADS):
            a = act[:, hh * HEAD:(hh + 1) * HEAD]
            a = a * lax.rsqrt(jnp.sum(a * a, axis=-1, keepdims=True) + EPS)
            if c == 0:
                a = a * (HEAD ** -0.5)
            o_ref[:, hh * HEAD:(hh + 1) * HEAD] = a

    def tile(t, hb):
        for g in range(3):
            epilogue = MIXER_LAYOUT[3 * t + g][2]
            out = jnp.dot(hb, w_ref[t, :, g * MIX:(g + 1) * MIX], preferred_element_type=F32)
            if epilogue == 'conv' and fuse_conv:
                conv_group(t, out)
                continue
            if epilogue == 'silu':
                out = _silu(out)
            elif epilogue == 'sigmoid':
                out = jax.nn.sigmoid(out)
            o_ref[:, g * MIX:(g + 1) * MIX] = out

    @pl.when(j == 0)
    def _():
        h = _rms(x_ref[...]) * nw_ref[...]
        h = h * (1.0 + sc_ref[...]) + sh_ref[...]
        hb = h.astype(BF16)
        h_scr[...] = hb
        ab_ref[...] = jnp.dot(hb, wab_ref[...], preferred_element_type=F32)
        if fuse_conv:
            @pl.when(i % tiles_per_seq == 0)
            def _():
                carry_scr[...] = jnp.zeros_like(carry_scr)
        tile(0, hb)

    for t in range(1, MIXER_TILES):
        @pl.when(j == t)
        def _(t=t):
            tile(t, h_scr[...])

    @pl.when(j >= MIXER_TILES)
    def _():
        acc = jnp.dot(h_scr[...], w_ref[j], preferred_element_type=F32)
        g_ref[...] = jax.nn.sigmoid(acc).astype(BF16)


def _in_call(x2, mod, per_token, seq_len, norm_w, w_main, w_ab, conv_w, fuse_conv, layer):
    T = x2.shape[0]
    tm = _row_tile(T if per_token else seq_len, 512 if per_token else 1024)
    tn = IN_TN
    tps = max(seq_len // tm, 1)
    last_gate = IN_TILES - MIXER_TILES - 1
    assert not fuse_conv or not per_token
    assert all(MIXER_LAYOUT[3 * c][0] == name for c, name in enumerate(CONV_GROUPS))

    def gate_block(i, j):
        on_gate = j >= MIXER_TILES
        row = jnp.where(on_gate, i, jnp.maximum(i - 1, 0))
        col = jnp.where(on_gate, j - MIXER_TILES, jnp.where(i > 0, last_gate, 0))
        return row, col

    in_specs = [
        pl.BlockSpec((tm, D_MODEL), lambda i, j: (i, 0)),
        _mod_spec(per_token, tm, tps, 0, 2),
        _mod_spec(per_token, tm, tps, 1, 2),
        pl.BlockSpec((1, D_MODEL), lambda i, j: (0, 0)),
        pl.BlockSpec((None, IN_TILES, D_MODEL, tn), lambda i, j: (layer, 0, 0, 0), pipeline_mode=pl.Buffered(1)),
        pl.BlockSpec((None, D_MODEL, AB_COLS), lambda i, j: (layer, 0, 0)),
    ]
    out_specs = [
        pl.BlockSpec((tm, tn), lambda i, j: (i, jnp.minimum(j, MIXER_TILES - 1))),
        pl.BlockSpec((tm, tn), gate_block),
        pl.BlockSpec((tm, AB_COLS), lambda i, j: (i, 0)),
    ]
    out_shape = [
        jax.ShapeDtypeStruct((T, MIXER_COLS), F32),
        jax.ShapeDtypeStruct((T, N_BRANCH * D_MODEL), BF16),
        jax.ShapeDtypeStruct((T, AB_COLS), F32),
    ]
    scratch = [pltpu.VMEM((tm, D_MODEL), BF16)]
    operands = [x2, mod, mod, norm_w.reshape(1, D_MODEL), w_main, w_ab]
    if fuse_conv:
        in_specs.append(pl.BlockSpec((CONV_W, CONV_DIM), lambda i, j: (0, 0)))
        operands.append(conv_w)
        out_specs.append(pl.BlockSpec((SUBLANES, CONV_DIM), lambda i, j: (i, 0)))
        out_shape.append(jax.ShapeDtypeStruct((T // tm * SUBLANES, CONV_DIM), F32))
        scratch += [pltpu.VMEM((tm + SUBLANES, MIX), F32), pltpu.VMEM((SUBLANES, CONV_DIM), F32)]
    return pl.pallas_call(
        functools.partial(_in_kernel, fuse_conv=fuse_conv, tiles_per_seq=tps),
        grid=(T // tm, IN_TILES),
        in_specs=in_specs,
        out_specs=out_specs,
        out_shape=out_shape,
        scratch_shapes=scratch,
        compiler_params=pltpu.CompilerParams(
            dimension_semantics=("arbitrary", "arbitrary"), vmem_limit_bytes=VMEM_LIMIT),
        name="in_proj",
    )(*operands)


def _mixer_kernel(*refs, C, Lv, nb, layer, passes, has_state, depth, pre_conv):
    proj_ref, ab_ref = refs[:2]
    pos = 2
    if has_state:
        conv0_ref, sg0_ref, sr0_ref, sh0_ref = refs[pos:pos + 4]
        pos += 4
    (cw_ref, hp_ref, gn_ref, hn_ref, lb_ref, cos_ref, sin_ref,
     dtab_ref, inner_ref, toend_ref, cdec_ref) = refs[pos:pos + 11]
    pos += 11 + (3 if layer > 0 else 0)
    ob_ref, sg_ref, sr_ref, sh_ref, hg_ref = refs[pos:pos + 5]
    xpad_ref = None if pre_conv else refs[-1]
    stacked = (sg_ref, sr_ref, sh_ref)
    if layer == 0:
        sg_ref, sr_ref, sh_ref = (r.at[0] for r in stacked)
    tail0 = SUBLANES - (CONV_W - 1)

    @pl.when(pl.program_id(1) == 0)
    def _():
        if layer == 0 and depth > 1:
            for r in stacked:
                r[1:] = jnp.zeros((depth - 1,) + r.shape[1:], F32)
        if has_state:
            sg_ref[...] = sg0_ref[...]
            sr_ref[...] = sr0_ref[...]
            sh_ref[...] = sh0_ref[...]
            xpad_ref[:, tail0:SUBLANES, :] = conv0_ref[...]
        else:
            sg_ref[...] = jnp.zeros_like(sg_ref)
            sr_ref[...] = jnp.zeros_like(sr_ref)
            sh_ref[...] = jnp.zeros_like(sh_ref)
            if not pre_conv:
                xpad_ref[:, tail0:SUBLANES, :] = jnp.zeros((nb, CONV_W - 1, CONV_DIM), F32)

    padded = Lv < C
    row1 = lax.broadcasted_iota(jnp.int32, (C, 1), 0)
    valid = row1 < Lv
    ri = lax.broadcasted_iota(jnp.int32, (C, C), 0)
    ci = lax.broadcasted_iota(jnp.int32, (C, C), 1)
    incl = ri >= ci
    strict = ri > ci
    eye = ri == ci
    ltri = jnp.where(incl, 1.0, 0.0).astype(BF16)
    r128 = lax.broadcasted_iota(jnp.int32, (HEAD, HEAD), 0)
    c128 = lax.broadcasted_iota(jnp.int32, (HEAD, HEAD), 1)
    eye128 = r128 == c128
    n_levels = max((Lv - 1).bit_length(), 1)

    lbp = lb_ref[...]
    lbe = jnp.exp(lbp - jnp.max(lbp, axis=0, keepdims=True))
    lbw = lbe / jnp.sum(lbe, axis=0, keepdims=True)
    lb_row = lbw[0:1, :]
    for d in range(1, layer + 1):
        lb_row = lb_row + lbw[d:d + 1, :]
    lb_row = lb_row - lbw[0:1, :]

    neg_a = -jnp.exp(hp_ref[0:1, :])
    dt_b = hp_ref[1:2, :]
    gn = gn_ref[...]
    hn = hn_ref[...]
    cosf = cos_ref[...]
    sinf = sin_ref[...]

    SB = SUBLANES

    def head_cols(n, name, h):
        return proj_ref[n, :, COL[name] + h * HEAD:COL[name] + (h + 1) * HEAD]

    def gdn_task(n, h, qkv, G_all, beta_all):
        if pre_conv:
            q = head_cols(n, 'gdn_q', h)
            k = head_cols(n, 'gdn_k', h)
            v = head_cols(n, 'gdn_v', h)
        else:
            q = qkv[:, h * HEAD:(h + 1) * HEAD]
            k = qkv[:, MIX + h * HEAD:MIX + (h + 1) * HEAD]
            v = qkv[:, 2 * MIX + h * HEAD:2 * MIX + (h + 1) * HEAD]
            q = q * lax.rsqrt(jnp.sum(q * q, axis=-1, keepdims=True) + EPS) * (HEAD ** -0.5)
            k = k * lax.rsqrt(jnp.sum(k * k, axis=-1, keepdims=True) + EPS)
        kk = _dot(k, k, 'nt', passes)
        qk = _dot(q, k, 'nt', passes)
        yield
        Gc = G_all[:, h:h + 1]
        beta = beta_all[:, N_HEADS + h:N_HEADS + h + 1]
        Gr = jnp.sum(jnp.where(eye, Gc, 0.0), axis=0, keepdims=True)
        G_last = G_all[C - 1:C, h:h + 1]
        eG = jnp.exp(Gc)
        decay = jnp.where(incl, jnp.exp(jnp.where(incl, Gc - Gr, 0.0)), 0.0)
        P = -jnp.where(strict, beta * kk * decay, 0.0)
        sol = jnp.concatenate([beta * v, (beta * eG) * k], axis=-1)
        for lvl in range(n_levels):
            upd = _dot(P, sol, 'nn', passes)
            if lvl + 1 < n_levels:
                P = _dot(P, P, 'nn', passes)
            yield
            sol = sol + upd
        S = sg_ref[n, h]
        sk_s = _dot(sol[:, HEAD:], S, 'nn', passes)
        q_s = _dot(q, S, 'nn', passes)
        yield
        u = sol[:, :HEAD] - sk_s
        k_end = k * jnp.exp(G_last - Gc)
        intra = _dot(qk * decay, u, 'nn', passes)
        s_upd = _dot(k_end, u, 'tn', passes)
        yield
        o = eG * q_s + intra
        sg_ref[n, h] = jnp.exp(G_last) * S + s_upd
        z_act = head_cols(n, 'gdn_z', h)
        ob_ref[n, :, h * HEAD:(h + 1) * HEAD] = ((_rms(o) * gn) * z_act).astype(ob_ref.dtype)

    def ret_task(n, h):
        q = head_cols(n, 'ret_q', h)
        k = head_cols(n, 'ret_k', h)
        v = head_cols(n, 'ret_v', h)
        q = q * cosf + pltpu.roll(q, HEAD // 2, 1) * sinf
        k = (k * cosf + pltpu.roll(k, HEAD // 2, 1) * sinf) * (HEAD ** -0.5)
        if padded:
            v = jnp.where(valid, v, 0.0)
        S = sr_ref[n, h]
        qk = _dot(q, k, 'nt', passes)
        q_s = _dot(q, S, 'nn', passes)
        s_upd = _dot(k * toend_ref[h], v, 'tn', passes)
        yield
        intra = _dot(qk * dtab_ref[h], v, 'nn', passes)
        sr_ref[n, h] = cdec_ref[h] * S + s_upd
        yield
        o = intra + inner_ref[h] * q_s
        g_act = head_cols(n, 'ret_g', h)
        mu = jnp.mean(o, axis=-1, keepdims=True)
        oc = o - mu
        var = jnp.mean(oc * oc, axis=-1, keepdims=True)
        ob_ref[n, :, MIX + h * HEAD:MIX + (h + 1) * HEAD] = (
            oc * lax.rsqrt(var + EPS) * g_act).astype(ob_ref.dtype)

    def hg_task(n, h):
        qc = head_cols(n, 'hg_q', h)
        zf = head_cols(n, 'hg_f', h)
        vc = head_cols(n, 'hg_i', h)
        lb_h = lb_row[:, h * HEAD:(h + 1) * HEAD]
        s_neg = jax.nn.sigmoid(-zf)
        f_gate = jax.nn.sigmoid(zf) + lb_h * s_neg
        log_f = jnp.log(jnp.maximum(f_gate, F_TINY))
        kc = (1.0 - lb_h) * s_neg
        if padded:
            log_f = jnp.where(valid, log_f, 0.0)
            kc = jnp.where(valid, kc, 0.0)
            vc = jnp.where(valid, vc, 0.0)
        A = _cumsum_rows(ltri, log_f)
        yield
        hgs = hg_ref.at[n, h]
        hgs[0] = A
        hgs[1] = kc
        S = sh_ref[n, h]
        a_last = hgs[0, C - 1:C, :]
        q_s = _dot(qc * jnp.exp(A), S, 'nn', passes)
        s_upd = _dot(kc * jnp.exp(a_last - A), vc, 'tn', passes)
        score_rows = []
        for blk in range(C // SB):
            r0 = blk * SB
            qI = qc[r0:r0 + SB]
            AI = A[r0:r0 + SB]
            rowg = r0 + lax.broadcasted_iota(jnp.int32, (SB, C), 0)
            colg = lax.broadcasted_iota(jnp.int32, (SB, C), 1)
            cross = None
            if r0 > 0:
                a_b = hgs[0, r0 - 1:r0, :]
                kt = jnp.concatenate([kc[:r0] * jnp.exp(a_b - A[:r0]), jnp.zeros((C - r0, HEAD), F32)], axis=0)
                qt = qI * jnp.exp(AI - a_b)
                cross = _dot(qt, kt, 'nt', passes)
            sc_blk = jnp.zeros((SB, C), F32)
            for s in range(SB):
                a_s = hgs[0, r0 + s:r0 + s + 1, :]
                k_s = hgs[1, r0 + s:r0 + s + 1, :]
                w = jnp.exp(AI - a_s)
                val = jnp.sum(qI * k_s * w, axis=-1, keepdims=True)
                sc_blk = jnp.where(colg == r0 + s, val, sc_blk)
            yield
            if cross is not None:
                sc_blk = sc_blk + cross
            score_rows.append(jnp.where(rowg >= colg, sc_blk, 0.0))
        scores = score_rows[0] if len(score_rows) == 1 else jnp.concatenate(score_rows, axis=0)
        intra = _dot(scores, vc, 'nn', passes)
        dec_col = jnp.sum(jnp.where(eye128, jnp.exp(a_last), 0.0), axis=1, keepdims=True)
        sh_ref[n, h] = dec_col * S + s_upd
        yield
        o = intra + q_s
        g_act = head_cols(n, 'hg_g', h)
        ob_ref[n, :, 2 * MIX + h * HEAD:2 * MIX + (h + 1) * HEAD] = (
            (_rms(o) * hn) * g_act).astype(ob_ref.dtype)

    stagger = STAGGER_ROUNDS if nb <= 2 else 0
    gdn_tasks, other_tasks = [], []
    for n in range(nb):
        qkv = None
        if not pre_conv:
            for c, name in enumerate(CONV_GROUPS):
                xpad_ref[n, SUBLANES:SUBLANES + C, c * MIX:(c + 1) * MIX] = proj_ref[n, :, COL[name]:COL[name] + MIX]
            conv = xpad_ref[n, SUBLANES:SUBLANES + C, :] * cw_ref[CONV_W - 1:CONV_W, :]
            for j in range(CONV_W - 1):
                off = SUBLANES - (CONV_W - 1) + j
                conv = conv + xpad_ref[n, off:off + C, :] * cw_ref[j:j + 1, :]
            xpad_ref[n, 0:SUBLANES, :] = xpad_ref[n, C:C + SUBLANES, :]
            qkv = _silu(conv)
        ab = ab_ref[n]
        g_all = neg_a * jax.nn.softplus(ab + dt_b)
        beta_all = jax.nn.sigmoid(ab)
        if padded:
            g_all = jnp.where(valid, g_all, 0.0)
            beta_all = jnp.where(valid, beta_all, 0.0)
        G_all = _cumsum_rows(ltri, g_all)
        for h in range(N_HEADS):
            gdn_tasks.append([n * stagger, gdn_task(n, h, qkv, G_all, beta_all)])
            other_tasks.append([n * stagger, hg_task(n, h)])
            other_tasks.append([n * stagger, ret_task(n, h)])
    tasks = gdn_tasks + other_tasks
    while tasks:
        alive = []
        for t in tasks:
            if t[0] > 0:
                t[0] -= 1
                alive.append(t)
                continue
            try:
                next(t[1])
                alive.append(t)
            except StopIteration:
                pass
        tasks = alive


def _mixer_call(proj3, ab3, states, stacked_so_far, conv_w, head_params, gdn_norm, hg_norm, hgrn_lb,
                cosf, sinf, dtab, inner, toend, cdec, *, C, Lv, nb, layer, passes, pre_conv):
    B, Lp, _ = proj3.shape
    n_chunks = Lp // C
    depth = hgrn_lb.shape[0]
    in_state_spec = pl.BlockSpec((None, nb, N_HEADS, HEAD, HEAD), lambda b, i: (layer, b, 0, 0, 0))
    if layer == 0:
        out_state_spec = pl.BlockSpec((depth, nb, N_HEADS, HEAD, HEAD), lambda b, i: (0, b, 0, 0, 0))
    else:
        out_state_spec = in_state_spec
    state_shape = jax.ShapeDtypeStruct((depth, B, N_HEADS, HEAD, HEAD), F32)
    ob_dtype = BF16 if C % (2 * SUBLANES) == 0 else F32

    def whole(shape):
        return pl.BlockSpec(shape, lambda b, i: (0,) * len(shape))

    has_state = states is not None
    operands = [proj3, ab3]
    in_specs = [
        pl.BlockSpec((nb, C, MIXER_COLS), lambda b, i: (b, i, 0)),
        pl.BlockSpec((nb, C, AB_COLS), lambda b, i: (b, i, 0)),
    ]
    if has_state:
        operands += [states[1], states[0], states[2], states[3]]
        in_specs += [pl.BlockSpec((None, nb, CONV_W - 1, CONV_DIM), lambda b, i: (layer, b, 0, 0)),
                     in_state_spec, in_state_spec, in_state_spec]
    operands += [conv_w, head_params, gdn_norm.reshape(1, HEAD), hg_norm.reshape(1, HEAD), hgrn_lb,
                 cosf, sinf, dtab, inner, toend, cdec]

    aliased = list(stacked_so_far) if layer > 0 else []
    aliases = {len(operands) + k: 1 + k for k in range(len(aliased))}
    kern = functools.partial(_mixer_kernel, C=C, Lv=Lv, nb=nb, layer=layer, passes=passes,
                             has_state=has_state, depth=depth, pre_conv=pre_conv)
    scratch = [pltpu.VMEM((nb, N_HEADS, 2, C, HEAD), F32)]
    if not pre_conv:
        scratch.append(pltpu.VMEM((nb, C + SUBLANES, CONV_DIM), F32))
    return pl.pallas_call(
        kern,
        grid=(B // nb, n_chunks),
        in_specs=in_specs + [
            whole((CONV_W, CONV_DIM)),
            whole((2, AB_COLS)),
            whole((1, HEAD)),
            whole((1, HEAD)),
            whole((depth, MIX)),
            pl.BlockSpec((C, HEAD), lambda b, i: (i, 0)),
            pl.BlockSpec((C, HEAD), lambda b, i: (i, 0)),
            whole((N_HEADS, C, C)),
            whole((N_HEADS, C, HEAD)),
            whole((N_HEADS, C, HEAD)),
            whole((N_HEADS, 1, HEAD)),
        ] + [pl.BlockSpec(memory_space=pl.ANY)] * len(aliased),
        out_specs=[
            pl.BlockSpec((nb, C, N_BRANCH * MIX), lambda b, i: (b, i, 0)),
            out_state_spec, out_state_spec, out_state_spec,
        ],
        out_shape=[
            jax.ShapeDtypeStruct((B, Lp, N_BRANCH * MIX), ob_dtype),
            state_shape, state_shape, state_shape,
        ],
        input_output_aliases=aliases,
        scratch_shapes=scratch,
        compiler_params=pltpu.CompilerParams(
            dimension_semantics=("arbitrary", "arbitrary"), vmem_limit_bytes=VMEM_LIMIT),
        name="mixer",
    )(*operands, *aliased)


FF_SPLITS = (0, 1024, 2048, D_FF)


def _post_ffn_kernel(ob_ref, mg_ref, x_ref, gtm_ref, sh_ref, sc_ref, gtf_ref, nw_ref, fw_ref,
                     wb_ref, wo_ref, wg_ref, wu_ref, wd_ref, o_ref, *, final):
    merged = None
    for n in range(N_BRANCH):
        br = _dot(ob_ref[:, n * MIX:(n + 1) * MIX], wb_ref[n])
        term = mg_ref[:, n * D_MODEL:(n + 1) * D_MODEL].astype(F32) * br
        merged = term if merged is None else merged + term
    x = x_ref[...] + gtm_ref[...] * _dot(merged, wo_ref[...])
    h = (_rms(x) * nw_ref[...] * (1.0 + sc_ref[...]) + sh_ref[...]).astype(BF16)
    y = None
    for lo, hi in zip(FF_SPLITS[:-1], FF_SPLITS[1:]):
        g = jnp.dot(h, wg_ref[:, lo:hi], preferred_element_type=F32)
        u = jnp.dot(h, wu_ref[:, lo:hi], preferred_element_type=F32)
        part = _dot(_silu(g) * u, wd_ref[lo:hi, :])
        y = part if y is None else y + part
    xn = x + gtf_ref[...] * y
    if final:
        xn = _rms(xn) * fw_ref[...]
    o_ref[...] = xn


def _post_ffn_call(ob2, gates, x2, mod, per_token, seq_len, norm_w, final_w, w_branch, w_out, w_gate_up, w_down,
                   final, layer):
    T = x2.shape[0]
    tm = _row_tile(T if per_token else seq_len, 256 if per_token else 512)
    tps = max(seq_len // tm, 1)
    resident = pl.Buffered(1)
    return pl.pallas_call(
        functools.partial(_post_ffn_kernel, final=final),
        grid=(T // tm,),
        in_specs=[
            pl.BlockSpec((tm, N_BRANCH * MIX), lambda i: (i, 0)),
            pl.BlockSpec((tm, N_BRANCH * D_MODEL), lambda i: (i, 0)),
            pl.BlockSpec((tm, D_MODEL), lambda i: (i, 0)),
            _mod_spec(per_token, tm, tps, 2, 1),
            _mod_spec(per_token, tm, tps, 3, 1),
            _mod_spec(per_token, tm, tps, 4, 1),
            _mod_spec(per_token, tm, tps, 5, 1),
            pl.BlockSpec((1, D_MODEL), lambda i: (0, 0)),
            pl.BlockSpec((1, D_MODEL), lambda i: (0, 0)),
            pl.BlockSpec((None, N_BRANCH, MIX, D_MODEL), lambda i: (layer, 0, 0, 0), pipeline_mode=resident),
            pl.BlockSpec((None, D_MODEL, D_MODEL), lambda i: (layer, 0, 0), pipeline_mode=resident),
            pl.BlockSpec((None, D_MODEL, D_FF), lambda i: (layer, 0, 0), pipeline_mode=resident),
            pl.BlockSpec((None, D_MODEL, D_FF), lambda i: (layer, 0, 1), pipeline_mode=resident),
            pl.BlockSpec((None, D_FF, D_MODEL), lambda i: (layer, 0, 0), pipeline_mode=resident),
        ],
        out_specs=pl.BlockSpec((tm, D_MODEL), lambda i: (i, 0)),
        out_shape=jax.ShapeDtypeStruct((T, D_MODEL), F32),
        compiler_params=pltpu.CompilerParams(
            dimension_semantics=("arbitrary",), vmem_limit_bytes=VMEM_LIMIT),
        name="post_ffn",
    )(ob2, gates, x2, mod, mod, mod, mod, norm_w.reshape(1, D_MODEL), final_w.reshape(1, D_MODEL),
      w_branch, w_out, w_gate_up, w_gate_up, w_down)


def _rope_tables(pos0, length, padded_len):
    half = HEAD // 2
    pos = pos0 + jnp.arange(padded_len, dtype=jnp.int32)
    inv = ROPE_BASE ** (-jnp.arange(half, dtype=F32) / half)
    ang = pos.astype(F32)[:, None] * inv[None, :]
    cos, sin = jnp.cos(ang), jnp.sin(ang)
    return jnp.concatenate([cos, cos], axis=-1), jnp.concatenate([-sin, sin], axis=-1)


def _retention_tables(c_real, c_pad):
    log_gamma = jnp.log1p(-jnp.exp2(-5.0 - jnp.arange(N_HEADS, dtype=F32)))
    idx = jnp.arange(c_real, dtype=F32)
    rel = idx[:, None] - idx[None, :]
    mask = rel >= 0
    lg = log_gamma[:, None, None]
    dtab = jnp.where(mask, jnp.exp(jnp.where(mask, lg * rel, 0.0)), 0.0)
    inner = jnp.exp(log_gamma[:, None] * (idx + 1.0))
    to_end = jnp.exp(log_gamma[:, None] * (c_real - 1.0 - idx))
    cdec = jnp.exp(log_gamma * c_real)
    p = c_pad - c_real
    dtab = jnp.pad(dtab, ((0, 0), (0, p), (0, p)))
    inner = jnp.broadcast_to(jnp.pad(inner, ((0, 0), (0, p)))[:, :, None], (N_HEADS, c_pad, HEAD))
    to_end = jnp.broadcast_to(jnp.pad(to_end, ((0, 0), (0, p)))[:, :, None], (N_HEADS, c_pad, HEAD))
    cdec = jnp.broadcast_to(cdec[:, None, None], (N_HEADS, 1, HEAD))
    return dtab, inner, to_end, cdec


def _trunk(x, mod_all, states, pos0, wts, nb, passes):
    B, L, _ = x.shape
    depth = mod_all.shape[0]
    c_real = CHUNK if L % CHUNK == 0 else L
    assert L % c_real == 0 and c_real <= CHUNK
    C = -(-c_real // SUBLANES) * SUBLANES
    Lp = L if C == c_real else C
    assert Lp == L or L == c_real
    per_token = Lp < 256
    fuse_conv = states is None and not per_token
    if Lp != L:
        x = jnp.pad(x, ((0, 0), (0, Lp - L), (0, 0)))
    T = B * Lp
    x2 = x.reshape(T, D_MODEL)
    cosf, sinf = _rope_tables(pos0, L, Lp)
    dtab, inner, toend, cdec = _retention_tables(c_real, C)

    tails, stacked = [], None
    for l in range(depth):
        last = l == depth - 1
        if per_token:
            mod = jnp.repeat(mod_all[l], Lp, axis=0)
        else:
            mod = mod_all[l].reshape(B, 1, 6 * D_MODEL)
        proj, gates, ab, *raw_tail = _in_call(x2, mod, per_token, Lp, wts['norm_mix'][l], wts['w_main'],
                                              wts['w_ab'], wts['conv_w'][l], fuse_conv, l)
        proj3 = proj.reshape(B, Lp, MIXER_COLS)
        ob, *stacked = _mixer_call(
            proj3, ab.reshape(B, Lp, AB_COLS), states, stacked,
            wts['conv_w'][l], wts['head_params'][l], wts['gdn_norm'][l], wts['hgrn_norm'][l], wts['hgrn_lb'],
            cosf, sinf, dtab, inner, toend, cdec, C=C, Lv=c_real, nb=nb, layer=l,
            passes=passes, pre_conv=fuse_conv)
        if fuse_conv:
            tail = raw_tail[0].reshape(B, -1, SUBLANES, CONV_DIM)[:, -1, SUBLANES - (CONV_W - 1):]
        else:
            tail = jnp.concatenate([proj3[:, L - min(L, CONV_W - 1):L, COL[name]:COL[name] + MIX]
                                    for name in CONV_GROUPS], axis=-1)
            if L < CONV_W - 1:
                tail = jnp.concatenate([states[1][l][:, L:], tail], axis=1)
        tails.append(tail)
        x2 = _post_ffn_call(ob.reshape(T, N_BRANCH * MIX), gates, x2, mod, per_token, Lp, wts['norm_ffn'][l],
                            wts['final_norm'], wts['w_branch'], wts['w_out'], wts['w_gate_up'],
                            wts['w_down'], final=last, layer=l)
    y = x2.reshape(B, Lp, D_MODEL)[:, :L]
    sg, sr, sh = stacked
    return y, (sg, jnp.stack(tails), sr, sh)


LANES = 128
GROUP_SRC = tuple(src for _, src, _ in MIXER_LAYOUT) + tuple(
    MERGE_SRC + k * MIX for k in range(N_BRANCH * D_MODEL // MIX))


def _regroup_kernel(a_tab, sh_tab, a_ref, b_ref, o_ref):
    del a_tab
    shifted = sh_tab[pl.program_id(1)] != 0

    @pl.when(shifted)
    def _():
        wide = jnp.concatenate([a_ref[...], b_ref[...]], axis=-1).astype(F32)
        skip = 2 * N_HEADS
        o_ref[...] = pltpu.roll(wide, MIX + LANES - skip, 1)[:, :MIX].astype(BF16)

    @pl.when(jnp.logical_not(shifted))
    def _():
        o_ref[...] = a_ref[...]


def _regroup_w_in(w_in):
    depth = w_in.shape[0]
    n_groups = len(GROUP_SRC)
    a_tab = jnp.asarray([src // MIX for src in GROUP_SRC], jnp.int32)
    sh_tab = jnp.asarray([src % MIX != 0 for src in GROUP_SRC], jnp.int32)
    assert all(src % MIX in (0, 2 * N_HEADS) for src in GROUP_SRC)
    per_tile = IN_TN // MIX
    grid_spec = pltpu.PrefetchScalarGridSpec(
        num_scalar_prefetch=2,
        grid=(depth, n_groups),
        in_specs=[
            pl.BlockSpec((None, D_MODEL, MIX), lambda l, g, a, s: (l, 0, a[g])),
            pl.BlockSpec((None, D_MODEL, LANES), lambda l, g, a, s: (l, 0, (a[g] + 1) * (MIX // LANES))),
        ],
        out_specs=pl.BlockSpec((None, None, D_MODEL, MIX), lambda l, g, a, s: (l, g // per_tile, 0, g % per_tile)),
    )
    return pl.pallas_call(
        _regroup_kernel,
        grid_spec=grid_spec,
        out_shape=jax.ShapeDtypeStruct((depth, IN_TILES, D_MODEL, IN_TN), BF16),
        compiler_params=pltpu.CompilerParams(
            dimension_semantics=("arbitrary", "arbitrary"), vmem_limit_bytes=VMEM_LIMIT),
        name="regroup_w_in",
    )(a_tab, sh_tab, w_in, w_in)


def _prep_weights(w_in, conv_w, gdn_a_log, gdn_dt_bias, gdn_norm, hgrn_lb, hgrn_norm, w_branch, w_out,
                  norm_mix, norm_ffn, w_gate_up, w_down, final_norm):
    w_main = _regroup_w_in(w_in.astype(BF16))
    w_ab = jnp.pad(w_in[:, :, AB_SRC:AB_SRC + 2 * N_HEADS],
                   ((0, 0), (0, 0), (0, AB_COLS - 2 * N_HEADS))).astype(BF16)
    head_params = jnp.pad(jnp.stack([gdn_a_log, gdn_dt_bias], axis=1),
                          ((0, 0), (0, 0), (0, AB_COLS - N_HEADS))).astype(F32)
    return dict(
        w_main=w_main, w_ab=w_ab, conv_w=conv_w, head_params=head_params, gdn_norm=gdn_norm,
        hgrn_norm=hgrn_norm, hgrn_lb=hgrn_lb, w_branch=w_branch.astype(BF16), w_out=w_out.astype(BF16),
        norm_mix=norm_mix, norm_ffn=norm_ffn, w_gate_up=w_gate_up.astype(BF16),
        w_down=w_down.astype(BF16), final_norm=final_norm)


def kernel(x_prompt, x_sample, state_gdn, state_gdn_conv, state_ret, state_hgrn, c_prompt, c_sample,
           w_in, conv_w, gdn_a_log, gdn_dt_bias, gdn_norm, hgrn_lb, hgrn_norm, w_branch, w_out,
           w_ada, b_ada, norm_mix, norm_ffn, w_gate_up, w_down, final_norm):
    Bp = x_prompt.shape[0]
    wts = _prep_weights(w_in, conv_w, gdn_a_log, gdn_dt_bias, gdn_norm, hgrn_lb, hgrn_norm, w_branch, w_out,
                        norm_mix, norm_ffn, w_gate_up, w_down, final_norm)
    c_all = jnp.concatenate([c_prompt, c_sample], axis=0)
    mod_all = _ada_call(c_all, w_ada, b_ada)
    y_p, st_p = _trunk(x_prompt, mod_all[:, :Bp], None, 0, wts, nb=2, passes=1)
    y_s, st_s = _trunk(x_sample, mod_all[:, Bp:], (state_gdn, state_gdn_conv, state_ret, state_hgrn),
                       PAST_LEN, wts, nb=8, passes=1)
    return (y_p, y_s) + st_p + st_s
```
